```python
import functools
import jax, jax.numpy as jnp
from jax import lax
import numpy as np

D_MODEL = 1024
BATCH = 8
SEQ = 4096
DEPTH = 1
DEC_BATCH = 128
DEC_SEQ = 4
PAST_LEN = 16384
PAGE_SIZE = 128

N_HEADS = 8
Q_LORA = 256
KV_LORA = 256
NOPE_DIM = 64
ROPE_DIM = 32
V_DIM = 64
ATTN_WIDTH = N_HEADS * V_DIM
ROPE_THETA = 10000.0
SM_SCALE = (NOPE_DIM + ROPE_DIM) ** -0.5
Q_BLOCK = 128
CHUNK = 128
GV = 512
N_GROUPS = 8
GROUP_DIM = GV // N_GROUPS
D_FF = 2816
N_SUB = 3
EPS = 1e-6
IN_SPLITS = (Q_LORA, KV_LORA, ROPE_DIM, GV, GV, D_MODEL, D_MODEL)
N_IN = Q_LORA + KV_LORA + ROPE_DIM + 2 * GV + 2 * D_MODEL

kernel_name = "mla_gmlp_macaron_adaln_step"


def rms_norm(x, g):
    xf = x.astype(jnp.float32)
    y = xf * lax.rsqrt(jnp.mean(xf * xf, axis=-1, keepdims=True) + EPS)
    return (y * g.astype(jnp.float32)).astype(x.dtype)


def layer_norm(x, g, b):
    xf = x.astype(jnp.float32)
    mu = jnp.mean(xf, axis=-1, keepdims=True)
    xc = xf - mu
    var = jnp.mean(xc * xc, axis=-1, keepdims=True)
    return (xc * lax.rsqrt(var + EPS) * g.astype(jnp.float32) + b.astype(jnp.float32)).astype(x.dtype)


def swiglu(h, w_gate, w_up, w_down):
    return (jax.nn.silu(h @ w_gate) * (h @ w_up)) @ w_down


def apply_rope(x, pos):
    half = ROPE_DIM // 2
    freqs = ROPE_THETA ** (-2.0 * jnp.arange(half, dtype=jnp.float32) / ROPE_DIM)
    ang = pos.astype(jnp.float32)[:, None] * freqs[None, :]
    shape = (ang.shape[0],) + (1,) * (x.ndim - 3) + (half,)
    cos = jnp.cos(ang).reshape(shape)
    sin = jnp.sin(ang).reshape(shape)
    xf = x.astype(jnp.float32)
    x1, x2 = xf[..., :half], xf[..., half:]
    return jnp.concatenate([x1 * cos - x2 * sin, x1 * sin + x2 * cos], axis=-1).astype(x.dtype)


def split_in(proj):
    idx, off = [], 0
    for s in IN_SPLITS[:-1]:
        off += s
        idx.append(off)
    return jnp.split(proj, idx, axis=-1)


def mla_scores(q_lat, q_pe, kv, k_pe):
    s = jnp.einsum('bqhc,bkc->bhqk', q_lat, kv) + jnp.einsum('bqhr,bkr->bhqk', q_pe, k_pe)
    return s.astype(jnp.float32) * SM_SCALE


def mla_prompt_attend(q_lat, q_pe, kv, k_pe):
    B, S = q_lat.shape[:2]
    nb = S // Q_BLOCK
    ql = q_lat.reshape(B, nb, Q_BLOCK, N_HEADS, KV_LORA).transpose(1, 0, 2, 3, 4)
    qp = q_pe.reshape(B, nb, Q_BLOCK, N_HEADS, ROPE_DIM).transpose(1, 0, 2, 3, 4)
    k_pos = jnp.arange(S)

    def block(args):
        i, ql_b, qp_b = args
        s = mla_scores(ql_b, qp_b, kv, k_pe)
        q_pos = i * Q_BLOCK + jnp.arange(Q_BLOCK)
        s = jnp.where(k_pos[None, :] <= q_pos[:, None], s, -jnp.inf)
        p = jax.nn.softmax(s, axis=-1).astype(kv.dtype)
        return jnp.einsum('bhqk,bkc->bqhc', p, kv)

    out = lax.map(block, (jnp.arange(nb), ql, qp))
    return out.transpose(1, 0, 2, 3, 4).reshape(B, S, N_HEADS, KV_LORA)


def mla_sample_attend(q_lat, q_pe, kv_new, pe_new, cache_kv, cache_pe, page_table, layer):
    DB, T = q_lat.shape[:2]
    past_kv = cache_kv[layer, page_table].reshape(DB, -1, KV_LORA)
    past_pe = cache_pe[layer, page_table].reshape(DB, -1, ROPE_DIM)
    n_past = past_kv.shape[1]
    s_past = mla_scores(q_lat, q_pe, past_kv, past_pe)
    s_new = mla_scores(q_lat, q_pe, kv_new, pe_new)
    s_new = jnp.where(jnp.tril(jnp.ones((T, T), dtype=bool)), s_new, -jnp.inf)
    p = jax.nn.softmax(jnp.concatenate([s_past, s_new], axis=-1), axis=-1).astype(kv_new.dtype)
    return (jnp.einsum('bhqk,bkc->bqhc', p[..., :n_past], past_kv)
            + jnp.einsum('bhqk,bkc->bqhc', p[..., n_past:], kv_new))


def spatial_gating(u, v_n, w_s, b_s, chunk_len):
    B, S, _ = u.shape
    tri = jnp.tril(jnp.ones((chunk_len, chunk_len), dtype=bool))
    w = jnp.where(tri, w_s[:, :chunk_len, :chunk_len], 0)
    vc = v_n.reshape(B, S // chunk_len, chunk_len, N_GROUPS, GROUP_DIM)
    mixed = jnp.einsum('gts,bnsgd->bntgd', w, vc) + b_s[:, :chunk_len].T[:, :, None]
    return u * mixed.reshape(B, S, GV)


def layer_forward(x, c, pos, attend, chunk_len, p):
    B, S, _ = x.shape
    mod = (jax.nn.silu(c) @ p['w_ada'] + p['b_ada']).reshape(B, N_SUB, 3, D_MODEL)
    shift, scale, gate = mod[:, :, 0, None, :], mod[:, :, 1, None, :], mod[:, :, 2, None, :]

    def modulated(h, g, i):
        return rms_norm(h, g) * (1 + scale[:, i]) + shift[:, i]

    x = x + 0.5 * gate[:, 0] * swiglu(modulated(x, p['g_ffn1'], 0), p['w1_gate'], p['w1_up'], p['w1_down'])

    n = modulated(x, p['g_mix'], 1)
    c_q, c_kv, k_pe, u, v, g_a, g_b = split_in(n @ p['w_in'])
    q = (rms_norm(c_q, p['g_q']) @ p['w_uq']).reshape(B, S, N_HEADS, NOPE_DIM + ROPE_DIM)
    q_pe = apply_rope(q[..., NOPE_DIM:], pos)
    q_lat = jnp.einsum('bshn,chn->bshc', q[..., :NOPE_DIM], p['w_uk'])
    kv = rms_norm(c_kv, p['g_kv'])
    k_pe = apply_rope(k_pe, pos)
    o_lat = attend(q_lat, q_pe, kv, k_pe)
    o_a = jnp.einsum('bshc,chv->bshv', o_lat, p['w_uv']).reshape(B, S, ATTN_WIDTH)
    u = jax.nn.gelu(u)
    v_n = layer_norm(jax.nn.gelu(v), p['ln_v_g'], p['ln_v_b'])
    o_b = spatial_gating(u, v_n, p['w_s'], p['b_s'], chunk_len)
    merged = jax.nn.sigmoid(g_a) * (o_a @ p['w_pa']) + jax.nn.sigmoid(g_b) * (o_b @ p['w_pb'])
    x = x + gate[:, 1] * (merged @ p['w_o'])

    x = x + 0.5 * gate[:, 2] * swiglu(modulated(x, p['g_ffn2'], 2), p['w2_gate'], p['w2_up'], p['w2_down'])
    return x, kv, k_pe, v_n


def setup_inputs(seed: int = 0) -> dict:
    key = jax.random.key(seed)
    ks = iter(jax.random.split(key, 48))

    def nrm(shape, scale):
        return jax.random.normal(next(ks), shape, jnp.float32) * scale

    def gain(shape):
        return 1.0 + 0.02 * jax.random.normal(next(ks), shape, jnp.float32)

    L = DEPTH
    n_pages = PAST_LEN // PAGE_SIZE
    n_used = DEC_BATCH * n_pages
    n_pool = n_used + max(1, n_used // 4)
    x_prompt = nrm((BATCH, SEQ, D_MODEL), 1.0)
    x_sample = nrm((DEC_BATCH, DEC_SEQ, D_MODEL), 1.0)
    cache_kv = nrm((L, n_pool, PAGE_SIZE, KV_LORA), 1.0)
    cache_pe = nrm((L, n_pool, PAGE_SIZE, ROPE_DIM), 1.0)
    page_table = jax.random.permutation(next(ks), n_pool)[:n_used].reshape(DEC_BATCH, n_pages).astype(jnp.int32)
    return {
        'x_prompt': x_prompt,
        'x_sample': x_sample,
        'cache_kv': cache_kv,
        'cache_pe': cache_pe,
        'page_table': page_table,
        'c_prompt': nrm((BATCH, D_MODEL), 1.0),
        'c_sample': nrm((DEC_BATCH, D_MODEL), 1.0),
        'w_ada': nrm((L, D_MODEL, N_SUB * 3 * D_MODEL), 0.5 * D_MODEL ** -0.5),
        'b_ada': nrm((L, N_SUB * 3 * D_MODEL), 0.02),
        'g_ffn1': gain((L, D_MODEL)),
        'w1_gate': nrm((L, D_MODEL, D_FF), D_MODEL ** -0.5),
        'w1_up': nrm((L, D_MODEL, D_FF), D_MODEL ** -0.5),
        'w1_down': nrm((L, D_FF, D_MODEL), D_FF ** -0.5),
        'g_mix': gain((L, D_MODEL)),
        'w_in': nrm((L, D_MODEL, N_IN), D_MODEL ** -0.5),
        'g_q': gain((L, Q_LORA)),
        'w_uq': nrm((L, Q_LORA, N_HEADS * (NOPE_DIM + ROPE_DIM)), Q_LORA ** -0.5),
        'g_kv': gain((L, KV_LORA)),
        'w_uk': nrm((L, KV_LORA, N_HEADS, NOPE_DIM), KV_LORA ** -0.5),
        'w_uv': nrm((L, KV_LORA, N_HEADS, V_DIM), KV_LORA ** -0.5),
        'ln_v_g': gain((L, GV)),
        'ln_v_b': nrm((L, GV), 0.02),
        'w_s': nrm((L, N_GROUPS, CHUNK, CHUNK), CHUNK ** -0.5),
        'b_s': gain((L, N_GROUPS, CHUNK)),
        'w_pa': nrm((L, ATTN_WIDTH, D_MODEL), ATTN_WIDTH ** -0.5),
        'w_pb': nrm((L, GV, D_MODEL), GV ** -0.5),
        'w_o': nrm((L, D_MODEL, D_MODEL), D_MODEL ** -0.5),
        'g_ffn2': gain((L, D_MODEL)),
        'w2_gate': nrm((L, D_MODEL, D_FF), D_MODEL ** -0.5),
        'w2_up': nrm((L, D_MODEL, D_FF), D_MODEL ** -0.5),
        'w2_down': nrm((L, D_FF, D_MODEL), D_FF ** -0.5),
        'g_final': gain((D_MODEL,)),
    }


def reference(x_prompt, x_sample, cache_kv, cache_pe, page_table, c_prompt, c_sample,
              w_ada, b_ada, g_ffn1, w1_gate, w1_up, w1_down, g_mix, w_in, g_q, w_uq,
              g_kv, w_uk, w_uv, ln_v_g, ln_v_b, w_s, b_s, w_pa, w_pb, w_o,
              g_ffn2, w2_gate, w2_up, w2_down, g_final):
    pos_p = jnp.arange(x_prompt.shape[1])
    pos_s = PAST_LEN + jnp.arange(x_sample.shape[1])
    h_p, h_s = x_prompt, x_sample
    kv_p, pe_p, kv_s, pe_s, gv_s = [], [], [], [], []
    for l in range(DEPTH):
        p = {
            'w_ada': w_ada[l], 'b_ada': b_ada[l],
            'g_ffn1': g_ffn1[l], 'w1_gate': w1_gate[l], 'w1_up': w1_up[l], 'w1_down': w1_down[l],
            'g_mix': g_mix[l], 'w_in': w_in[l], 'g_q': g_q[l], 'w_uq': w_uq[l],
            'g_kv': g_kv[l], 'w_uk': w_uk[l], 'w_uv': w_uv[l],
            'ln_v_g': ln_v_g[l], 'ln_v_b': ln_v_b[l], 'w_s': w_s[l], 'b_s': b_s[l],
            'w_pa': w_pa[l], 'w_pb': w_pb[l], 'w_o': w_o[l],
            'g_ffn2': g_ffn2[l], 'w2_gate': w2_gate[l], 'w2_up': w2_up[l], 'w2_down': w2_down[l],
        }
        h_p, kv, pe, _ = layer_forward(h_p, c_prompt, pos_p, mla_prompt_attend, CHUNK, p)
        kv_p.append(kv)
        pe_p.append(pe)
        attend_s = functools.partial(mla_sample_attend, cache_kv=cache_kv, cache_pe=cache_pe,
                                     page_table=page_table, layer=l)
        h_s, kv, pe, gv = layer_forward(h_s, c_sample, pos_s, attend_s, x_sample.shape[1], p)
        kv_s.append(kv)
        pe_s.append(pe)
        gv_s.append(gv)
    y_prompt = rms_norm(h_p, g_final)
    y_sample = rms_norm(h_s, g_final)
    return (y_prompt, y_sample, jnp.stack(kv_p), jnp.stack(pe_p), jnp.stack(kv_s), jnp.stack(pe_s), jnp.stack(gv_s))
```

```python
import functools

import jax
import jax.numpy as jnp
from jax import lax
from jax.experimental import pallas as pl
from jax.experimental.pallas import tpu as pltpu

D_MODEL = 1024
N_HEADS = 8
Q_LORA = 256
KV_LORA = 256
NOPE_DIM = 64
ROPE_DIM = 32
V_DIM = 64
ROPE_THETA = 10000.0
SM_SCALE = (NOPE_DIM + ROPE_DIM) ** -0.5
CHUNK = 128
GV = 512
N_GROUPS = 8
GROUP_DIM = GV // N_GROUPS
D_FF = 2816
N_SUB = 3
EPS = 1e-6
PAGE_SIZE = 128
PE_WIDE = N_HEADS * ROPE_DIM

VMEM_LIMIT_BYTES = 56 * 1024 * 1024
LANES = 128

BF16 = jnp.bfloat16
F32 = jnp.float32

_OFF_CQ = 0
_OFF_CKV = _OFF_CQ + Q_LORA
_OFF_KPE = _OFF_CKV + KV_LORA
_OFF_U = _OFF_KPE + PE_WIDE
_OFF_V = _OFF_U + GV
_OFF_GA = _OFF_V + GV
_OFF_GB = _OFF_GA + D_MODEL
_N_IN_R = _OFF_GB + D_MODEL


def _dot(a, b):
    return jnp.dot(a, b, preferred_element_type=F32)


def _dot_nt(a, b):
    return lax.dot_general(a, b, (((1,), (1,)), ((), ())), preferred_element_type=F32)


def _rms(x, g):
    return x * lax.rsqrt(jnp.mean(x * x, axis=-1, keepdims=True) + EPS) * g


def _mod_row(mod_ref, k, per_row):
    if per_row:
        return mod_ref[k]
    return mod_ref[k:k + 1, :]


def _const_spec(shape):
    nd = len(shape)
    return pl.BlockSpec(shape, lambda *_: (0,) * nd, pipeline_mode=pl.Buffered(1))


def _params(n_grid):
    return pltpu.CompilerParams(
        dimension_semantics=("arbitrary",) * n_grid,
        vmem_limit_bytes=VMEM_LIMIT_BYTES,
    )


def _ada_kernel(c_ref, w_ref, b_ref, o_ref):
    c = c_ref[...]
    a = (c * jax.nn.sigmoid(c)).astype(BF16)
    o_ref[...] = _dot(a, w_ref[...].astype(BF16)) + b_ref[...]


def _ada(c_all, w_ada, b_ada):
    rows, d = c_all.shape
    n = w_ada.shape[1]
    tn = 1024
    return pl.pallas_call(
        _ada_kernel,
        out_shape=jax.ShapeDtypeStruct((rows, n), F32),
        grid=(n // tn,),
        in_specs=[
            pl.BlockSpec((rows, d), lambda j: (0, 0)),
            pl.BlockSpec((d, tn), lambda j: (0, j)),
            pl.BlockSpec((1, tn), lambda j: (0, j)),
        ],
        out_specs=pl.BlockSpec((rows, tn), lambda j: (0, j)),
        compiler_params=_params(1),
        name="ada",
    )(c_all, w_ada, b_ada.reshape(1, n))


_FF_CHUNKS = ((0, 1024), (1024, 1024), (2048, 768))


def _ffn_kernel(x_ref, mod_ref, g_ref, wg_ref, wu_ref, wd_ref, gf_ref, o_ref, *, per_row, final_norm):
    x = x_ref[...]
    shift = _mod_row(mod_ref, 0, per_row)
    scale = _mod_row(mod_ref, 1, per_row)
    gate = _mod_row(mod_ref, 2, per_row)
    nb = (_rms(x, g_ref[...]) * (1.0 + scale) + shift).astype(BF16)
    acc = jnp.zeros(x.shape, F32)
    for off, width in _FF_CHUNKS:
        hg = _dot(nb, wg_ref[:, off:off + width])
        hu = _dot(nb, wu_ref[:, off:off + width])
        h = (hg * jax.nn.sigmoid(hg) * hu).astype(BF16)
        acc = acc + _dot(h, wd_ref[off:off + width, :])
    y = x + 0.5 * gate * acc
    if final_norm:
        y = _rms(y, gf_ref[...])
    o_ref[...] = y


def _mod_spec(per_row, tm):
    if per_row:
        return pl.BlockSpec((3, tm, D_MODEL), lambda g, i: (0, i, 0))
    return pl.BlockSpec((None, 3, D_MODEL), lambda g, i: (g, 0, 0))


def _row_spec(tm, width):
    return pl.BlockSpec((None, tm, width), lambda g, i: (g, i, 0))


def _ffn(x, mod, g, wg, wu, wd, g_final, *, per_row, final_norm, tm):
    G, T, _ = x.shape
    return pl.pallas_call(
        functools.partial(_ffn_kernel, per_row=per_row, final_norm=final_norm),
        out_shape=jax.ShapeDtypeStruct(x.shape, F32),
        grid=(G, T // tm),
        in_specs=[
            _row_spec(tm, D_MODEL),
            _mod_spec(per_row, tm),
            _const_spec((1, D_MODEL)),
            _const_spec(wg.shape),
            _const_spec(wu.shape),
            _const_spec(wd.shape),
            _const_spec((1, D_MODEL)),
        ],
        out_specs=_row_spec(tm, D_MODEL),
        compiler_params=_params(2),
        name="ffn",
    )(x, mod, g, wg, wu, wd, g_final)


def _rope(x, cos2, sin2):
    halves = []
    for k in range(x.shape[1] // LANES):
        xs = x[:, k * LANES:(k + 1) * LANES]
        lane = lax.broadcasted_iota(jnp.int32, xs.shape, 1)
        first = (lane % ROPE_DIM) < (ROPE_DIM // 2)
        swapped = jnp.where(first, pltpu.roll(xs, LANES - ROPE_DIM // 2, 1), pltpu.roll(xs, ROPE_DIM // 2, 1))
        halves.append(swapped)
    swapped = jnp.concatenate(halves, axis=1)
    return x * cos2 + swapped * sin2


def _mixin_kernel(x_ref, mod_ref, g_ref, win_ref, gq_ref, wuq_ref, wuk_ref, gkv_ref, cos_ref, sin_ref,
                  lng_ref, lnb_ref, wmix_ref, bmix_ref, wpb_ref,
                  kv_ref, pe_ref, kcat_ref, qlat_ref, qpe_ref, sa_ref, mb_ref, gv_ref,
                  *, per_row, chunk_len):
    x = x_ref[...]
    tm = x.shape[0]
    shift = _mod_row(mod_ref, 0, per_row)
    scale = _mod_row(mod_ref, 1, per_row)
    nb = (_rms(x, g_ref[...]) * (1.0 + scale) + shift).astype(BF16)
    cos2 = cos_ref[...]
    sin2 = sin_ref[...]

    def proj(off, width):
        return _dot(nb, win_ref[:, off:off + width])

    cqn = _rms(proj(_OFF_CQ, Q_LORA), gq_ref[...]).astype(BF16)
    q = _dot(cqn, wuq_ref[...])
    for h in range(N_HEADS):
        pair = q[:, (h // 2) * LANES:(h // 2 + 1) * LANES].astype(BF16)
        qlat_ref[h] = _dot(pair, wuk_ref[h]).astype(BF16)
    qpe_ref[...] = _rope(q[:, N_HEADS * NOPE_DIM:], cos2, sin2).astype(BF16)

    kv = _rms(proj(_OFF_CKV, KV_LORA), gkv_ref[...])
    kv_ref[...] = kv
    kpe = _rope(proj(_OFF_KPE, PE_WIDE), cos2, sin2)
    pe_ref[...] = kpe[:, :ROPE_DIM]
    kcat_ref[...] = jnp.concatenate([kv.astype(BF16), kpe.astype(BF16)], axis=1)

    u = jax.nn.gelu(proj(_OFF_U, GV))
    v = jax.nn.gelu(proj(_OFF_V, GV))
    mu = jnp.mean(v, axis=-1, keepdims=True)
    vc = v - mu
    var = jnp.mean(vc * vc, axis=-1, keepdims=True)
    v_n = vc * lax.rsqrt(var + EPS) * lng_ref[...] + lnb_ref[...]
    if gv_ref is not None:
        gv_ref[...] = v_n
    vnb = v_n.astype(BF16)
    row = lax.broadcasted_iota(jnp.int32, (CHUNK, CHUNK), 0)
    col = lax.broadcasted_iota(jnp.int32, (CHUNK, CHUNK), 1)
    keep = (col <= row) & ((row // chunk_len) == (col // chunk_len))
    wm = [jnp.where(keep, wmix_ref[g], 0.0).astype(BF16) for g in range(N_GROUPS)]
    lane = lax.broadcasted_iota(jnp.int32, (CHUNK, LANES), 1)
    low = lane < GROUP_DIM
    bias = bmix_ref[...]
    tiles = []
    for c in range(tm // CHUNK):
        cols = []
        for jp in range(GV // LANES):
            vv = vnb[c * CHUNK:(c + 1) * CHUNK, jp * LANES:(jp + 1) * LANES]
            cols.append(jnp.where(low, _dot(wm[2 * jp], vv), _dot(wm[2 * jp + 1], vv)))
        tiles.append(jnp.concatenate(cols, axis=1) + bias)
    mixed = jnp.concatenate(tiles, axis=0)
    o_b = (u * mixed).astype(BF16)
    mb_ref[...] = jax.nn.sigmoid(proj(_OFF_GB, D_MODEL)) * _dot(o_b, wpb_ref[...])
    sa_ref[...] = jax.nn.sigmoid(proj(_OFF_GA, D_MODEL))


def _mixin(x, mod, g_mix, w_in_r, g_q, w_uq_r, w_uk_p, g_kv, cos2, sin2, ln_g, ln_b, w_mix, b_mix, w_pb,
           *, per_row, chunk_len, want_gv, tm):
    G, T, _ = x.shape
    grid = (G, T // tm)
    out_shape = [
        jax.ShapeDtypeStruct((G, T, KV_LORA), F32),
        jax.ShapeDtypeStruct((G, T, ROPE_DIM), F32),
        jax.ShapeDtypeStruct((G, T, KV_LORA + PE_WIDE), BF16),
        jax.ShapeDtypeStruct((G, N_HEADS, T, KV_LORA), BF16),
        jax.ShapeDtypeStruct((G, T, PE_WIDE), BF16),
        jax.ShapeDtypeStruct((G, T, D_MODEL), F32),
        jax.ShapeDtypeStruct((G, T, D_MODEL), F32),
    ]
    out_specs = [
        _row_spec(tm, KV_LORA),
        _row_spec(tm, ROPE_DIM),
        _row_spec(tm, KV_LORA + PE_WIDE),
        pl.BlockSpec((None, N_HEADS, tm, KV_LORA), lambda g, i: (g, 0, i, 0)),
        _row_spec(tm, PE_WIDE),
        _row_spec(tm, D_MODEL),
        _row_spec(tm, D_MODEL),
    ]
    if want_gv:
        out_shape.append(jax.ShapeDtypeStruct((G, T, GV), F32))
        out_specs.append(_row_spec(tm, GV))
        body = functools.partial(_mixin_kernel, per_row=per_row, chunk_len=chunk_len)
    else:
        def body(*refs):
            _mixin_kernel(*refs, None, per_row=per_row, chunk_len=chunk_len)
    return pl.pallas_call(
        body,
        out_shape=out_shape,
        grid=grid,
        in_specs=[
            _row_spec(tm, D_MODEL),
            _mod_spec(per_row, tm),
            _const_spec((1, D_MODEL)),
            _const_spec(w_in_r.shape),
            _const_spec((1, Q_LORA)),
            _const_spec(w_uq_r.shape),
            _const_spec(w_uk_p.shape),
            _const_spec((1, KV_LORA)),
            pl.BlockSpec((tm, PE_WIDE), lambda g, i: (i, 0)),
            pl.BlockSpec((tm, PE_WIDE), lambda g, i: (i, 0)),
            _const_spec((1, GV)),
            _const_spec((1, GV)),
            _const_spec(w_mix.shape),
            _const_spec(b_mix.shape),
            _const_spec(w_pb.shape),
        ],
        out_specs=out_specs,
        compiler_params=_params(2),
        name="mixin",
    )(x, mod, g_mix, w_in_r, g_q, w_uq_r, w_uk_p, g_kv, cos2, sin2, ln_g, ln_b, w_mix, b_mix, w_pb)


_TQ = 128
_TK = 256


def _pattn_kernel(qlat_ref, qpe_ref, kcat_ref, o_ref, m_sc, l_sc, acc_sc):
    i = pl.program_id(1)
    rows = N_HEADS * _TQ
    ql = qlat_ref[...].reshape(rows, KV_LORA)
    qp = qpe_ref[...]
    head_of_lane = lax.broadcasted_iota(jnp.int32, qp.shape, 1) // ROPE_DIM
    qps = jnp.concatenate([jnp.where(head_of_lane == h, qp, jnp.zeros_like(qp)) for h in range(N_HEADS)], axis=0)
    q = jnp.concatenate([ql, qps], axis=1)
    m_sc[...] = jnp.full(m_sc.shape, -1e30, F32)
    l_sc[...] = jnp.zeros(l_sc.shape, F32)
    acc_sc[...] = jnp.zeros(acc_sc.shape, F32)

    def step(j, masked):
        k = kcat_ref[pl.ds(pl.multiple_of(j * _TK, _TK), _TK), :]
        s = _dot_nt(q, k) * SM_SCALE
        if masked:
            q_pos = i * _TQ + lax.broadcasted_iota(jnp.int32, s.shape, 0) % _TQ
            k_pos = j * _TK + lax.broadcasted_iota(jnp.int32, s.shape, 1)
            s = jnp.where(k_pos <= q_pos, s, -jnp.inf)
        m_prev = m_sc[...]
        m_new = jnp.maximum(m_prev, jnp.max(s, axis=-1, keepdims=True))
        alpha = jnp.exp(m_prev - m_new)
        p = jnp.exp(s - m_new)
        l_sc[...] = alpha * l_sc[...] + jnp.sum(p, axis=-1, keepdims=True)
        acc_sc[...] = alpha * acc_sc[...] + _dot(p.astype(BF16), k[:, :KV_LORA])
        m_sc[...] = m_new

    n_full = (i * _TQ) // _TK

    def body(j, carry):
        step(j, False)
        return carry

    lax.fori_loop(0, n_full, body, 0)
    step(n_full, True)
    o = acc_sc[...] / l_sc[...]
    o_ref[...] = o.reshape(N_HEADS, _TQ, KV_LORA).astype(BF16)


def _pattn(q_lat, q_pe, kcat):
    B, H, S, C = q_lat.shape
    rows = H * _TQ
    return pl.pallas_call(
        _pattn_kernel,
        out_shape=jax.ShapeDtypeStruct((B, H, S, C), BF16),
        grid=(B, S // _TQ),
        in_specs=[
            pl.BlockSpec((None, H, _TQ, C), lambda b, i: (b, 0, i, 0)),
            pl.BlockSpec((None, _TQ, PE_WIDE), lambda b, i: (b, i, 0)),
            pl.BlockSpec((None, S, C + PE_WIDE), lambda b, i: (b, 0, 0)),
        ],
        out_specs=pl.BlockSpec((None, H, _TQ, C), lambda b, i: (b, 0, i, 0)),
        scratch_shapes=[
            pltpu.VMEM((rows, 1), F32),
            pltpu.VMEM((rows, 1), F32),
            pltpu.VMEM((rows, C), F32),
        ],
        compiler_params=_params(2),
        name="pattn",
    )(q_lat, q_pe, kcat)


_PAGES_PER_CHUNK = 16


def _sattn_kernel(pt_ref, ql_ref, qp_ref, kvn_ref, pen_ref, ckv_hbm, cpe_hbm, o_ref,
                  kvbuf, pebuf, sem, *, n_pages, t_new):
    b = pl.program_id(0)
    pc = _PAGES_PER_CHUNK
    n_chunks = n_pages // pc
    rows = ql_ref.shape[0]

    def copies(c, slot):
        out = []
        for p in range(pc):
            page = pt_ref[b, c * pc + p]
            dst = pl.ds(p * PAGE_SIZE, PAGE_SIZE)
            out.append(pltpu.make_async_copy(ckv_hbm.at[page], kvbuf.at[slot, dst], sem.at[0, slot]))
            out.append(pltpu.make_async_copy(cpe_hbm.at[page], pebuf.at[slot, dst], sem.at[1, slot]))
        return out

    def start(c, slot):
        for cp in copies(c, slot):
            cp.start()

    def wait(c, slot):
        for cp in copies(c, slot):
            cp.wait()

    ql = ql_ref[...]
    qp = qp_ref[...]
    start(0, 0)

    def body(c, carry):
        m_prev, l_prev, acc = carry
        slot = c % 2

        @pl.when(c + 1 < n_chunks)
        def _():
            start(c + 1, 1 - slot)

        wait(c, slot)
        kvb = kvbuf[slot].astype(BF16)
        peb = pebuf[slot].astype(BF16)
        s = (_dot_nt(ql, kvb) + _dot_nt(qp, peb)) * SM_SCALE
        m_new = jnp.maximum(m_prev, jnp.max(s, axis=-1, keepdims=True))
        alpha = jnp.exp(m_prev - m_new)
        p = jnp.exp(s - m_new)
        l_new = alpha * l_prev + jnp.sum(p, axis=-1, keepdims=True)
        acc = alpha * acc + _dot(p.astype(BF16), kvb)
        return m_new, l_new, acc

    init = (jnp.full((rows, 1), -1e30, F32), jnp.zeros((rows, 1), F32), jnp.zeros((rows, KV_LORA), F32))
    m_prev, l_prev, acc = lax.fori_loop(0, n_chunks, body, init)

    qlf = ql.astype(F32)
    qpf = qp.astype(F32)
    kvn = kvn_ref[...]
    pen = pen_ref[...]
    t_of_row = lax.broadcasted_iota(jnp.int32, (rows, 1), 0) % t_new
    s_new = []
    for t in range(t_new):
        st = (jnp.sum(qlf * kvn[t:t + 1, :], axis=-1, keepdims=True)
              + jnp.sum(qpf * pen[t:t + 1, :], axis=-1, keepdims=True)) * SM_SCALE
        s_new.append(jnp.where(t <= t_of_row, st, -jnp.inf))
    m_new = m_prev
    for st in s_new:
        m_new = jnp.maximum(m_new, st)
    alpha = jnp.exp(m_prev - m_new)
    l_new = alpha * l_prev
    acc = alpha * acc
    for t, st in enumerate(s_new):
        pt = jnp.exp(st - m_new)
        l_new = l_new + pt
        acc = acc + pt * kvn[t:t + 1, :]
    o_ref[...] = acc / l_new


def _sattn(page_table, q_lat, q_pe, kv_new, pe_new, cache_kv, cache_pe):
    DB, rows, C = q_lat.shape
    t_new = kv_new.shape[1]
    n_pages = page_table.shape[1]
    pc = _PAGES_PER_CHUNK
    grid_spec = pltpu.PrefetchScalarGridSpec(
        num_scalar_prefetch=1,
        grid=(DB,),
        in_specs=[
            pl.BlockSpec((None, rows, C), lambda b, pt: (b, 0, 0)),
            pl.BlockSpec((None, rows, ROPE_DIM), lambda b, pt: (b, 0, 0)),
            pl.BlockSpec((None, t_new, C), lambda b, pt: (b, 0, 0)),
            pl.BlockSpec((None, t_new, ROPE_DIM), lambda b, pt: (b, 0, 0)),
            pl.BlockSpec(memory_space=pl.ANY),
            pl.BlockSpec(memory_space=pl.ANY),
        ],
        out_specs=pl.BlockSpec((None, rows, C), lambda b, pt: (b, 0, 0)),
        scratch_shapes=[
            pltpu.VMEM((2, pc * PAGE_SIZE, C), F32),
            pltpu.VMEM((2, pc * PAGE_SIZE, ROPE_DIM), F32),
            pltpu.SemaphoreType.DMA((2, 2)),
        ],
    )
    return pl.pallas_call(
        functools.partial(_sattn_kernel, n_pages=n_pages, t_new=t_new),
        out_shape=jax.ShapeDtypeStruct((DB, rows, C), F32),
        grid_spec=grid_spec,
        compiler_params=_params(1),
        name="sattn",
    )(page_table, q_lat, q_pe, kv_new, pe_new, cache_kv, cache_pe)


def _merge_kernel(x_ref, ol_ref, sa_ref, mb_ref, mod_ref, wuv_ref, wpa_ref, wo_ref, o_ref, *, per_row):
    gate = _mod_row(mod_ref, 2, per_row)
    parts = []
    for jp in range(N_HEADS // 2):
        a = _dot(ol_ref[2 * jp], wuv_ref[2 * jp]) + _dot(ol_ref[2 * jp + 1], wuv_ref[2 * jp + 1])
        parts.append(a.astype(BF16))
    o_a = jnp.concatenate(parts, axis=1)
    merged = sa_ref[...] * _dot(o_a, wpa_ref[...]) + mb_ref[...]
    o_ref[...] = x_ref[...] + gate * _dot(merged.astype(BF16), wo_ref[...])


def _merge(x, o_lat, sa, mb, mod, w_uv_p, w_pa, w_o, *, per_row, tm):
    G, T, _ = x.shape
    return pl.pallas_call(
        functools.partial(_merge_kernel, per_row=per_row),
        out_shape=jax.ShapeDtypeStruct(x.shape, F32),
        grid=(G, T // tm),
        in_specs=[
            _row_spec(tm, D_MODEL),
            pl.BlockSpec((None, N_HEADS, tm, KV_LORA), lambda g, i: (g, 0, i, 0)),
            _row_spec(tm, D_MODEL),
            _row_spec(tm, D_MODEL),
            _mod_spec(per_row, tm),
            _const_spec(w_uv_p.shape),
            _const_spec(w_pa.shape),
            _const_spec(w_o.shape),
        ],
        out_specs=_row_spec(tm, D_MODEL),
        compiler_params=_params(2),
        name="merge",
    )(x, o_lat, sa, mb, mod, w_uv_p, w_pa, w_o)


def _rope_tables(pos):
    half = ROPE_DIM // 2
    freqs = ROPE_THETA ** (-2.0 * jnp.arange(half, dtype=F32) / ROPE_DIM)
    ang = pos.astype(F32)[:, None] * freqs[None, :]
    cos = jnp.cos(ang)
    sin = jnp.sin(ang)
    cos2 = jnp.tile(jnp.concatenate([cos, cos], axis=1), (1, N_HEADS))
    sin2 = jnp.tile(jnp.concatenate([-sin, sin], axis=1), (1, N_HEADS))
    return cos2, sin2


def _pair_pad(w, axis):
    z = jnp.zeros_like(w)
    even = jnp.concatenate([w, z], axis=axis)
    odd = jnp.concatenate([z, w], axis=axis)
    sel = (jnp.arange(w.shape[0]) % 2 == 0).reshape(-1, 1, 1)
    return jnp.where(sel, even, odd)


def kernel(x_prompt, x_sample, cache_kv, cache_pe, page_table, c_prompt, c_sample, w_ada, b_ada, g_ffn1, w1_gate, w1_up, w1_down, g_mix, w_in, g_q, w_uq, g_kv, w_uk, w_uv, ln_v_g, ln_v_b, w_s, b_s, w_pa, w_pb, w_o, g_ffn2, w2_gate, w2_up, w2_down, g_final):
    B, S, D = x_prompt.shape
    DB, T, _ = x_sample.shape
    depth = w_ada.shape[0]
    assert depth == 1
    l = 0
    R = DB * T

    row = lambda v: v.reshape(1, -1)
    wi = w_in[l]
    seg = lambda a, n: wi[:, a:a + n]
    o_ckv = Q_LORA
    o_kpe = o_ckv + KV_LORA
    o_u = o_kpe + ROPE_DIM
    o_v = o_u + GV
    o_ga = o_v + GV
    o_gb = o_ga + D_MODEL
    w_in_r = jnp.concatenate([
        seg(0, Q_LORA), seg(o_ckv, KV_LORA), jnp.tile(seg(o_kpe, ROPE_DIM), (1, N_HEADS)),
        seg(o_u, GV), seg(o_v, GV), seg(o_ga, D_MODEL), seg(o_gb, D_MODEL)], axis=1).astype(BF16)
    wq = w_uq[l].reshape(Q_LORA, N_HEADS, NOPE_DIM + ROPE_DIM)
    w_uq_r = jnp.concatenate([wq[:, :, :NOPE_DIM].reshape(Q_LORA, -1),
                              wq[:, :, NOPE_DIM:].reshape(Q_LORA, -1)], axis=1).astype(BF16)
    w_uk_p = _pair_pad(jnp.transpose(w_uk[l], (1, 2, 0)), 1).astype(BF16)
    w_uv_p = _pair_pad(jnp.transpose(w_uv[l], (1, 0, 2)), 2).astype(BF16)
    bf = lambda w: w[l].astype(BF16)
    w1g, w1u, w1d = bf(w1_gate), bf(w1_up), bf(w1_down)
    w2g, w2u, w2d = bf(w2_gate), bf(w2_up), bf(w2_down)
    wpa, wpb, wo = bf(w_pa), bf(w_pb), bf(w_o)
    gf = row(g_final)

    w_mix_p = w_s[l]
    b_mix_p = jnp.repeat(b_s[l].T, GROUP_DIM, axis=1)
    reps = CHUNK // T
    w_mix_s = jnp.tile(w_s[l][:, :T, :T], (1, reps, reps))
    b_mix_s = jnp.repeat(jnp.tile(b_s[l][:, :T].T, (reps, 1)), GROUP_DIM, axis=1)

    mod = _ada(jnp.concatenate([c_prompt, c_sample], axis=0), w_ada[l], b_ada[l])
    mod_p = mod[:B].reshape(B, N_SUB, 3, D)
    mod_s = jnp.transpose(jnp.repeat(mod[B:], T, axis=0).reshape(R, N_SUB, 3, D), (1, 2, 0, 3))

    past_len = page_table.shape[1] * PAGE_SIZE
    cos_p, sin_p = _rope_tables(jnp.arange(S))
    cos_s, sin_s = _rope_tables(past_len + jnp.arange(R) % T)

    def layer(x, mods, per_row, tm, chunk_len, cos2, sin2, w_mix, b_mix, want_gv, attend):
        x1 = _ffn(x, mods(0), row(g_ffn1[l]), w1g, w1u, w1d, gf, per_row=per_row, final_norm=False, tm=tm)
        outs = _mixin(x1, mods(1), row(g_mix[l]), w_in_r, row(g_q[l]), w_uq_r, w_uk_p, row(g_kv[l]),
                      cos2, sin2, row(ln_v_g[l]), row(ln_v_b[l]), w_mix, b_mix, wpb,
                      per_row=per_row, chunk_len=chunk_len, want_gv=want_gv, tm=tm)
        kv, pe, kcat, q_lat, q_pe, sa, mb = outs[:7]
        o_lat = attend(q_lat, q_pe, kcat, kv, pe)
        x2 = _merge(x1, o_lat, sa, mb, mods(1), w_uv_p, wpa, wo, per_row=per_row, tm=tm)
        y = _ffn(x2, mods(2), row(g_ffn2[l]), w2g, w2u, w2d, gf, per_row=per_row, final_norm=True, tm=tm)
        return y, kv, pe, (outs[7] if want_gv else None)

    def attend_p(q_lat, q_pe, kcat, kv, pe):
        return _pattn(q_lat, q_pe, kcat)

    y_p, kv_p, pe_p, _ = layer(x_prompt, lambda i: mod_p[:, i], False, 512, CHUNK, cos_p, sin_p,
                               w_mix_p, b_mix_p, False, attend_p)

    def attend_s(q_lat, q_pe, kcat, kv, pe):
        ql = jnp.transpose(q_lat.reshape(N_HEADS, DB, T, KV_LORA), (1, 0, 2, 3)).reshape(DB, N_HEADS * T, KV_LORA)
        qp = jnp.transpose(q_pe.reshape(DB, T, N_HEADS, ROPE_DIM), (0, 2, 1, 3)).reshape(DB, N_HEADS * T, ROPE_DIM)
        o = _sattn(page_table, ql, qp, kv.reshape(DB, T, KV_LORA), pe.reshape(DB, T, ROPE_DIM),
                   cache_kv[l], cache_pe[l])
        o = jnp.transpose(o.reshape(DB, N_HEADS, T, KV_LORA), (1, 0, 2, 3)).reshape(1, N_HEADS, R, KV_LORA)
        return o.astype(BF16)

    y_s, kv_s, pe_s, gv_s = layer(x_sample.reshape(1, R, D), lambda i: mod_s[i], True, R, T, cos_s, sin_s,
                                  w_mix_s, b_mix_s, True, attend_s)

    return (y_p, y_s.reshape(DB, T, D),
            kv_p.reshape(1, B, S, KV_LORA), pe_p.reshape(1, B, S, ROPE_DIM),
            kv_s.reshape(1, DB, T, KV_LORA), pe_s.reshape(1, DB, T, ROPE_DIM),
            gv_s.reshape(1, DB, T, GV))
```

```python
import functools

import jax
import jax.numpy as jnp
from jax import lax
from jax.experimental import pallas as pl
from jax.experimental.pallas import tpu as pltpu

D_MODEL = 1024
N_HEADS = 8
Q_LORA = 256
KV_LORA = 256
NOPE_DIM = 64
ROPE_DIM = 32
V_DIM = 64
ROPE_THETA = 10000.0
SM_SCALE = (NOPE_DIM + ROPE_DIM) ** -0.5
CHUNK = 128
GV = 512
N_GROUPS = 8
GROUP_DIM = GV // N_GROUPS
D_FF = 2816
N_SUB = 3
EPS = 1e-6
PAGE_SIZE = 128
PE_WIDE = N_HEADS * ROPE_DIM
KEY_CHUNK = 256

VMEM_LIMIT_BYTES = 56 * 1024 * 1024
LANES = 128

BF16 = jnp.bfloat16
F32 = jnp.float32

_OFF_CQ = 0
_OFF_CKV = _OFF_CQ + Q_LORA
_OFF_KPE = _OFF_CKV + KV_LORA
_OFF_U = _OFF_KPE + PE_WIDE
_OFF_V = _OFF_U + GV
_OFF_GA = _OFF_V + GV
_OFF_GB = _OFF_GA + D_MODEL
_N_IN_R = _OFF_GB + D_MODEL


def _dot(a, b):
    return jnp.dot(a, b, preferred_element_type=F32)


def _dot_nt(a, b):
    return lax.dot_general(a, b, (((1,), (1,)), ((), ())), preferred_element_type=F32)


def _rms(x, g):
    return x * lax.rsqrt(jnp.mean(x * x, axis=-1, keepdims=True) + EPS) * g


def _mod_row(mod_ref, k, per_row):
    if per_row:
        return mod_ref[k]
    return mod_ref[k:k + 1, :]


def _const_spec(shape):
    nd = len(shape)
    return pl.BlockSpec(shape, lambda *_: (0,) * nd, pipeline_mode=pl.Buffered(1))


def _params(n_grid):
    return pltpu.CompilerParams(
        dimension_semantics=("arbitrary",) * n_grid,
        vmem_limit_bytes=VMEM_LIMIT_BYTES,
    )


def _ada_kernel(c_ref, w_ref, b_ref, o_ref):
    c = c_ref[...]
    a = (c * jax.nn.sigmoid(c)).astype(BF16)
    o_ref[...] = _dot(a, w_ref[...].astype(BF16)) + b_ref[...]


def _ada(c_all, w_ada, b_ada):
    rows, d = c_all.shape
    n = w_ada.shape[1]
    tn = 1024
    return pl.pallas_call(
        _ada_kernel,
        out_shape=jax.ShapeDtypeStruct((rows, n), F32),
        grid=(n // tn,),
        in_specs=[
            pl.BlockSpec((rows, d), lambda j: (0, 0)),
            pl.BlockSpec((d, tn), lambda j: (0, j)),
            pl.BlockSpec((1, tn), lambda j: (0, j)),
        ],
        out_specs=pl.BlockSpec((rows, tn), lambda j: (0, j)),
        compiler_params=_params(1),
        name="ada",
    )(c_all, w_ada, b_ada.reshape(1, n))


_FF_CHUNKS = ((0, 1024), (1024, 1024), (2048, 768))


def _ffn_kernel(x_ref, mod_ref, g_ref, wg_ref, wu_ref, wd_ref, gf_ref, o_ref, *, per_row, final_norm):
    x = x_ref[...]
    shift = _mod_row(mod_ref, 0, per_row)
    scale = _mod_row(mod_ref, 1, per_row)
    gate = _mod_row(mod_ref, 2, per_row)
    nb = (_rms(x, g_ref[...]) * (1.0 + scale) + shift).astype(BF16)
    acc = jnp.zeros(x.shape, F32)
    for off, width in _FF_CHUNKS:
        hg = _dot(nb, wg_ref[:, off:off + width])
        hu = _dot(nb, wu_ref[:, off:off + width])
        h = (hg * jax.nn.sigmoid(hg) * hu).astype(BF16)
        acc = acc + _dot(h, wd_ref[off:off + width, :])
    y = x + 0.5 * gate * acc
    if final_norm:
        y = _rms(y, gf_ref[...])
    o_ref[...] = y


def _mod_spec(per_row, tm):
    if per_row:
        return pl.BlockSpec((3, tm, D_MODEL), lambda g, i: (0, i, 0))
    return pl.BlockSpec((None, 3, D_MODEL), lambda g, i: (g, 0, 0))


def _row_spec(tm, width):
    return pl.BlockSpec((None, tm, width), lambda g, i: (g, i, 0))


def _ffn(x, mod, g, wg, wu, wd, g_final, *, per_row, final_norm, tm):
    G, T, _ = x.shape
    return pl.pallas_call(
        functools.partial(_ffn_kernel, per_row=per_row, final_norm=final_norm),
        out_shape=jax.ShapeDtypeStruct(x.shape, F32),
        grid=(G, T // tm),
        in_specs=[
            _row_spec(tm, D_MODEL),
            _mod_spec(per_row, tm),
            _const_spec((1, D_MODEL)),
            _const_spec(wg.shape),
            _const_spec(wu.shape),
            _const_spec(wd.shape),
            _const_spec((1, D_MODEL)),
        ],
        out_specs=_row_spec(tm, D_MODEL),
        compiler_params=_params(2),
        name="ffn",
    )(x, mod, g, wg, wu, wd, g_final)


def _rope(x, cos2, sin2):
    halves = []
    for k in range(x.shape[1] // LANES):
        xs = x[:, k * LANES:(k + 1) * LANES]
        lane = lax.broadcasted_iota(jnp.int32, xs.shape, 1)
        first = (lane % ROPE_DIM) < (ROPE_DIM // 2)
        swapped = jnp.where(first, pltpu.roll(xs, LANES - ROPE_DIM // 2, 1), pltpu.roll(xs, ROPE_DIM // 2, 1))
        halves.append(swapped)
    swapped = jnp.concatenate(halves, axis=1)
    return x * cos2 + swapped * sin2


def _mixin_kernel(x_ref, mod_ref, g_ref, win_ref, gq_ref, wuq_ref, wuk_ref, gkv_ref, cos_ref, sin_ref,
                  lng_ref, lnb_ref, wmix_ref, bmix_ref, wpb_ref,
                  kv_ref, pe_ref, kcat_ref, kvt_ref, qlat_ref, qpe_ref, sa_ref, mb_ref, gv_ref,
                  *, per_row, chunk_len):
    x = x_ref[...]
    tm = x.shape[0]
    shift = _mod_row(mod_ref, 0, per_row)
    scale = _mod_row(mod_ref, 1, per_row)
    nb = (_rms(x, g_ref[...]) * (1.0 + scale) + shift).astype(BF16)
    cos2 = cos_ref[...]
    sin2 = sin_ref[...]

    def proj(off, width):
        return _dot(nb, win_ref[:, off:off + width])

    cqn = _rms(proj(_OFF_CQ, Q_LORA), gq_ref[...]).astype(BF16)
    q = _dot(cqn, wuq_ref[...])
    for h in range(N_HEADS):
        pair = q[:, (h // 2) * LANES:(h // 2 + 1) * LANES].astype(BF16)
        qlat_h = _dot(pair, wuk_ref[h])
        qlat_ref[h] = qlat_h.T.astype(BF16)
    qpe_ref[...] = _rope(q[:, N_HEADS * NOPE_DIM:], cos2, sin2).T.astype(BF16)

    kv = _rms(proj(_OFF_CKV, KV_LORA), gkv_ref[...])
    kv_ref[...] = kv
    kpe = _rope(proj(_OFF_KPE, PE_WIDE), cos2, sin2)
    pe_ref[...] = kpe[:, :ROPE_DIM]
    kcat_ref[...] = jnp.concatenate([kv.astype(BF16), kpe.astype(BF16)], axis=1)
    kv_t = kv.T.astype(BF16)
    for c in range(tm // KEY_CHUNK):
        kvt_ref[c] = kv_t[:, c * KEY_CHUNK:(c + 1) * KEY_CHUNK]

    u = jax.nn.gelu(proj(_OFF_U, GV))
    v = jax.nn.gelu(proj(_OFF_V, GV))
    mu = jnp.mean(v, axis=-1, keepdims=True)
    vc = v - mu
    var = jnp.mean(vc * vc, axis=-1, keepdims=True)
    v_n = vc * lax.rsqrt(var + EPS) * lng_ref[...] + lnb_ref[...]
    if gv_ref is not None:
        gv_ref[...] = v_n
    vnb = v_n.astype(BF16)
    row = lax.broadcasted_iota(jnp.int32, (CHUNK, CHUNK), 0)
    col = lax.broadcasted_iota(jnp.int32, (CHUNK, CHUNK), 1)
    keep = (col <= row) & ((row // chunk_len) == (col // chunk_len))
    wm = [jnp.where(keep, wmix_ref[g], 0.0).astype(BF16) for g in range(N_GROUPS)]
    lane = lax.broadcasted_iota(jnp.int32, (CHUNK, LANES), 1)
    low = lane < GROUP_DIM
    bias = bmix_ref[...]
    tiles = []
    for c in range(tm // CHUNK):
        cols = []
        for jp in range(GV // LANES):
            vv = vnb[c * CHUNK:(c + 1) * CHUNK, jp * LANES:(jp + 1) * LANES]
            cols.append(jnp.where(low, _dot(wm[2 * jp], vv), _dot(wm[2 * jp + 1], vv)))
        tiles.append(jnp.concatenate(cols, axis=1) + bias)
    mixed = jnp.concatenate(tiles, axis=0)
    o_b = (u * mixed).astype(BF16)
    mb_ref[...] = jax.nn.sigmoid(proj(_OFF_GB, D_MODEL)) * _dot(o_b, wpb_ref[...])
    sa_ref[...] = jax.nn.sigmoid(proj(_OFF_GA, D_MODEL))


def _mixin(x, mod, g_mix, w_in_r, g_q, w_uq_r, w_uk_p, g_kv, cos2, sin2, ln_g, ln_b, w_mix, b_mix, w_pb,
           *, per_row, chunk_len, want_gv, tm):
    G, T, _ = x.shape
    grid = (G, T // tm)
    out_shape = [
        jax.ShapeDtypeStruct((G, T, KV_LORA), F32),
        jax.ShapeDtypeStruct((G, T, ROPE_DIM), F32),
        jax.ShapeDtypeStruct((G, T, KV_LORA + PE_WIDE), BF16),
        jax.ShapeDtypeStruct((G, T // KEY_CHUNK, KV_LORA, KEY_CHUNK), BF16),
        jax.ShapeDtypeStruct((G, N_HEADS, KV_LORA, T), BF16),
        jax.ShapeDtypeStruct((G, PE_WIDE, T), BF16),
        jax.ShapeDtypeStruct((G, T, D_MODEL), F32),
        jax.ShapeDtypeStruct((G, T, D_MODEL), F32),
    ]
    out_specs = [
        _row_spec(tm, KV_LORA),
        _row_spec(tm, ROPE_DIM),
        _row_spec(tm, KV_LORA + PE_WIDE),
        pl.BlockSpec((None, tm // KEY_CHUNK, KV_LORA, KEY_CHUNK), lambda g, i: (g, i, 0, 0)),
        pl.BlockSpec((None, N_HEADS, KV_LORA, tm), lambda g, i: (g, 0, 0, i)),
        pl.BlockSpec((None, PE_WIDE, tm), lambda g, i: (g, 0, i)),
        _row_spec(tm, D_MODEL),
        _row_spec(tm, D_MODEL),
    ]
    if want_gv:
        out_shape.append(jax.ShapeDtypeStruct((G, T, GV), F32))
        out_specs.append(_row_spec(tm, GV))
        body = functools.partial(_mixin_kernel, per_row=per_row, chunk_len=chunk_len)
    else:
        def body(*refs):
            _mixin_kernel(*refs, None, per_row=per_row, chunk_len=chunk_len)
    return pl.pallas_call(
        body,
        out_shape=out_shape,
        grid=grid,
        in_specs=[
            _row_spec(tm, D_MODEL),
            _mod_spec(per_row, tm),
            _const_spec((1, D_MODEL)),
            _const_spec(w_in_r.shape),
            _const_spec((1, Q_LORA)),
            _const_spec(w_uq_r.shape),
            _const_spec(w_uk_p.shape),
            _const_spec((1, KV_LORA)),
            pl.BlockSpec((tm, PE_WIDE), lambda g, i: (i, 0)),
            pl.BlockSpec((tm, PE_WIDE), lambda g, i: (i, 0)),
            _const_spec((1, GV)),
            _const_spec((1, GV)),
            _const_spec(w_mix.shape),
            _const_spec(b_mix.shape),
            _const_spec(w_pb.shape),
        ],
        out_specs=out_specs,
        compiler_params=_params(2),
        name="mixin",
    )(x, mod, g_mix, w_in_r, g_q, w_uq_r, w_uk_p, g_kv, cos2, sin2, ln_g, ln_b, w_mix, b_mix, w_pb)


_TQ = 128
_HEADS_PER_BLOCK = 2
_COL_BLOCK = _HEADS_PER_BLOCK * _TQ


def _pattn_kernel(qlat_ref, qpe_ref, kcat_ref, kvt_ref, o_ref, qt_sc, s0_sc, s1_sc, m_sc, l_sc, acc_sc):
    i = pl.program_id(1)
    tk = KEY_CHUNK
    head_of_row = lax.broadcasted_iota(jnp.int32, (PE_WIDE, _TQ), 0) // ROPE_DIM
    qp = qpe_ref[...]
    for h in range(N_HEADS):
        qt_sc[0:KV_LORA, h * _TQ:(h + 1) * _TQ] = qlat_ref[h]
        qt_sc[KV_LORA:, h * _TQ:(h + 1) * _TQ] = jnp.where(head_of_row == h, qp, jnp.zeros_like(qp))
    m_sc[...] = jnp.full(m_sc.shape, -1e30, F32)
    l_sc[...] = jnp.zeros(l_sc.shape, F32)
    acc_sc[...] = jnp.zeros(acc_sc.shape, F32)

    def scores(j, s_ref):
        k = kcat_ref[pl.ds(pl.multiple_of(j * tk, tk), tk), :]
        s_ref[...] = _dot(k, qt_sc[...])

    def accumulate(j, s_ref, masked):
        v_t = kvt_ref[j]
        for cb in range(N_HEADS // _HEADS_PER_BLOCK):
            cols = slice(cb * _COL_BLOCK, (cb + 1) * _COL_BLOCK)
            s = s_ref[:, cols] * SM_SCALE
            if masked:
                q_pos = i * _TQ + lax.broadcasted_iota(jnp.int32, s.shape, 1) % _TQ
                k_pos = j * tk + lax.broadcasted_iota(jnp.int32, s.shape, 0)
                s = jnp.where(k_pos <= q_pos, s, -jnp.inf)
            m_prev = m_sc[:, cols]
            m_new = jnp.maximum(m_prev, jnp.max(s, axis=0, keepdims=True))
            alpha = jnp.exp(m_prev - m_new)
            p = jnp.exp(s - m_new)
            l_sc[:, cols] = alpha * l_sc[:, cols] + jnp.sum(p, axis=0, keepdims=True)
            acc_sc[:, cols] = alpha * acc_sc[:, cols] + _dot(v_t, p.astype(BF16))
            m_sc[:, cols] = m_new

    n_full = (i * _TQ) // tk
    scores(0, s0_sc)

    def pair(p, carry):
        j = 2 * p
        scores(j + 1, s1_sc)
        accumulate(j, s0_sc, False)
        scores(j + 2, s0_sc)
        accumulate(j + 1, s1_sc, False)
        return carry

    lax.fori_loop(0, n_full // 2, pair, 0)

    @pl.when(n_full % 2 == 0)
    def _():
        accumulate(n_full, s0_sc, True)

    @pl.when(n_full % 2 == 1)
    def _():
        scores(n_full, s1_sc)
        accumulate(n_full - 1, s0_sc, False)
        accumulate(n_full, s1_sc, True)

    o_t = acc_sc[...] / l_sc[...]
    for h in range(N_HEADS):
        o_ref[h] = o_t[:, h * _TQ:(h + 1) * _TQ].T.astype(BF16)


def _pattn(q_lat_t, q_pe_t, kcat, kv_t):
    B, H, C, S = q_lat_t.shape
    cols = H * _TQ
    return pl.pallas_call(
        _pattn_kernel,
        out_shape=jax.ShapeDtypeStruct((B, H, S, C), BF16),
        grid=(B, S // _TQ),
        in_specs=[
            pl.BlockSpec((None, H, C, _TQ), lambda b, i: (b, 0, 0, i)),
            pl.BlockSpec((None, PE_WIDE, _TQ), lambda b, i: (b, 0, i)),
            pl.BlockSpec((None, S, C + PE_WIDE), lambda b, i: (b, 0, 0)),
            pl.BlockSpec((None, S // KEY_CHUNK, C, KEY_CHUNK), lambda b, i: (b, 0, 0, 0)),
        ],
        out_specs=pl.BlockSpec((None, H, _TQ, C), lambda b, i: (b, 0, i, 0)),
        scratch_shapes=[
            pltpu.VMEM((C + PE_WIDE, cols), BF16),
            pltpu.VMEM((KEY_CHUNK, cols), F32),
            pltpu.VMEM((KEY_CHUNK, cols), F32),
            pltpu.VMEM((1, cols), F32),
            pltpu.VMEM((1, cols), F32),
            pltpu.VMEM((C, cols), F32),
        ],
        compiler_params=_params(2),
        name="pattn",
    )(q_lat_t, q_pe_t, kcat, kv_t)


_PAGES_PER_CHUNK = 16


_DMA_SLOTS = 3


def _sattn_kernel(pt_ref, ql_ref, qp_ref, kvn_ref, pen_ref, ckv_hbm, cpet_hbm, o_ref,
                  kvbuf, pebuf, kvb16_0, kvb16_1, s0_sc, s1_sc, m_sc, l_sc, acc_sc, sem, *, n_pages, t_new):
    pc = _PAGES_PER_CHUNK
    n_chunks = n_pages // pc
    n_seq, rows, _ = ql_ref.shape
    total = n_seq * n_chunks
    assert n_chunks % 2 == 0 and total > _DMA_SLOTS
    work = ((kvb16_0, s0_sc), (kvb16_1, s1_sc))

    def copies(g, slot):
        b = g // n_chunks
        c = g % n_chunks
        out = []
        for p in range(pc):
            page = pt_ref[b, c * pc + p]
            dst = pl.ds(p * PAGE_SIZE, PAGE_SIZE)
            out.append(pltpu.make_async_copy(ckv_hbm.at[page], kvbuf.at[slot, dst], sem.at[0, slot]))
            out.append(pltpu.make_async_copy(cpet_hbm.at[page], pebuf.at[slot, :, dst], sem.at[1, slot]))
        return out

    def start(g):
        for cp in copies(g, g % _DMA_SLOTS):
            cp.start()

    def wait(g):
        for cp in copies(g, g % _DMA_SLOTS):
            cp.wait()

    def scores(g, w):
        kvb16, s_sc = work[w]
        b = g // n_chunks
        slot = g % _DMA_SLOTS
        wait(g)
        kvb = kvbuf[slot].astype(BF16)
        kvb16[...] = kvb
        peb = pebuf[slot].astype(BF16)
        s_sc[...] = (_dot_nt(ql_ref[b], kvb) + _dot(qp_ref[b], peb)) * SM_SCALE

    def reset_state():
        m_sc[...] = jnp.full(m_sc.shape, -1e30, F32)
        l_sc[...] = jnp.zeros(l_sc.shape, F32)
        acc_sc[...] = jnp.zeros(acc_sc.shape, F32)

    def accumulate(w):
        kvb16, s_sc = work[w]
        s = s_sc[...]
        m_prev = m_sc[...]
        m_new = jnp.maximum(m_prev, jnp.max(s, axis=-1, keepdims=True))
        alpha = jnp.exp(m_prev - m_new)
        p = jnp.exp(s - m_new)
        l_sc[...] = alpha * l_sc[...] + jnp.sum(p, axis=-1, keepdims=True)
        acc_sc[...] = alpha * acc_sc[...] + _dot(p.astype(BF16), kvb16[...])
        m_sc[...] = m_new

    def finish(b):
        qlf = ql_ref[b].astype(F32)
        qpf = qp_ref[b].astype(F32)
        kvn = kvn_ref[b]
        pen = pen_ref[b]
        t_of_row = lax.broadcasted_iota(jnp.int32, (rows, 1), 0) % t_new
        s_new = []
        for t in range(t_new):
            st = (jnp.sum(qlf * kvn[t:t + 1, :], axis=-1, keepdims=True)
                  + jnp.sum(qpf * pen[t:t + 1, :], axis=-1, keepdims=True)) * SM_SCALE
            s_new.append(jnp.where(t <= t_of_row, st, -jnp.inf))
        m_prev = m_sc[...]
        m_new = m_prev
        for st in s_new:
            m_new = jnp.maximum(m_new, st)
        alpha = jnp.exp(m_prev - m_new)
        l_new = alpha * l_sc[...]
        acc = alpha * acc_sc[...]
        for t, st in enumerate(s_new):
            pt = jnp.exp(st - m_new)
            l_new = l_new + pt
            acc = acc + pt * kvn[t:t + 1, :]
        o_ref[b] = acc / l_new
        reset_state()

    reset_state()
    for g0 in range(_DMA_SLOTS):
        start(g0)
    scores(0, 0)

    def body(k, carry):
        g = 2 * k + 1
        start(g + _DMA_SLOTS - 1)
        scores(g, 1)
        accumulate(0)
        start(g + _DMA_SLOTS)
        scores(g + 1, 0)
        accumulate(1)

        @pl.when((g + 1) % n_chunks == 0)
        def _():
            finish((g + 1) // n_chunks - 1)

        return carry

    lax.fori_loop(0, total // 2 - 1 + pl.program_id(0), body, 0)
    scores(total - 1, 1)
    accumulate(0)
    accumulate(1)
    finish(n_seq - 1)
    wait(total)


def _sattn(page_table, q_lat, q_pe, kv_new, pe_new, cache_kv, cache_pe_t):
    DB, rows, C = q_lat.shape
    t_new = kv_new.shape[1]
    n_pages = page_table.shape[1]
    keys = _PAGES_PER_CHUNK * PAGE_SIZE
    page_table = jnp.concatenate([page_table, page_table[:1]], axis=0)
    full = lambda shape: pl.BlockSpec(shape, lambda i, pt: (0,) * len(shape))
    grid_spec = pltpu.PrefetchScalarGridSpec(
        num_scalar_prefetch=1,
        grid=(1,),
        in_specs=[
            full(q_lat.shape),
            full(q_pe.shape),
            full(kv_new.shape),
            full(pe_new.shape),
            pl.BlockSpec(memory_space=pl.ANY),
            pl.BlockSpec(memory_space=pl.ANY),
        ],
        out_specs=full((DB, rows, C)),
        scratch_shapes=[
            pltpu.VMEM((_DMA_SLOTS, keys, C), F32),
            pltpu.VMEM((_DMA_SLOTS, ROPE_DIM, keys), F32),
            pltpu.VMEM((keys, C), BF16),
            pltpu.VMEM((keys, C), BF16),
            pltpu.VMEM((rows, keys), F32),
            pltpu.VMEM((rows, keys), F32),
            pltpu.VMEM((rows, 1), F32),
            pltpu.VMEM((rows, 1), F32),
            pltpu.VMEM((rows, C), F32),
            pltpu.SemaphoreType.DMA((2, _DMA_SLOTS)),
        ],
    )
    return pl.pallas_call(
        functools.partial(_sattn_kernel, n_pages=n_pages, t_new=t_new),
        out_shape=jax.ShapeDtypeStruct((DB, rows, C), F32),
        grid_spec=grid_spec,
        compiler_params=_params(1),
        name="sattn",
    )(page_table, q_lat, q_pe, kv_new, pe_new, cache_kv, cache_pe_t)


def _merge_kernel(x_ref, ol_ref, sa_ref, mb_ref, mod_ref, wuv_ref, wpa_ref, wo_ref, o_ref, *, per_row):
    gate = _mod_row(mod_ref, 2, per_row)
    parts = []
    for jp in range(N_HEADS // 2):
        a = _dot(ol_ref[2 * jp], wuv_ref[2 * jp]) + _dot(ol_ref[2 * jp + 1], wuv_ref[2 * jp + 1])
        parts.append(a.astype(BF16))
    o_a = jnp.concatenate(parts, axis=1)
    merged = sa_ref[...] * _dot(o_a, wpa_ref[...]) + mb_ref[...]
    o_ref[...] = x_ref[...] + gate * _dot(merged.astype(BF16), wo_ref[...])


def _merge(x, o_lat, sa, mb, mod, w_uv_p, w_pa, w_o, *, per_row, tm):
    G, T, _ = x.shape
    return pl.pallas_call(
        functools.partial(_merge_kernel, per_row=per_row),
        out_shape=jax.ShapeDtypeStruct(x.shape, F32),
        grid=(G, T // tm),
        in_specs=[
            _row_spec(tm, D_MODEL),
            pl.BlockSpec((None, N_HEADS, tm, KV_LORA), lambda g, i: (g, 0, i, 0)),
            _row_spec(tm, D_MODEL),
            _row_spec(tm, D_MODEL),
            _mod_spec(per_row, tm),
            _const_spec(w_uv_p.shape),
            _const_spec(w_pa.shape),
            _const_spec(w_o.shape),
        ],
        out_specs=_row_spec(tm, D_MODEL),
        compiler_params=_params(2),
        name="merge",
    )(x, o_lat, sa, mb, mod, w_uv_p, w_pa, w_o)


def _rope_tables(pos):
    half = ROPE_DIM // 2
    freqs = ROPE_THETA ** (-2.0 * jnp.arange(half, dtype=F32) / ROPE_DIM)
    ang = pos.astype(F32)[:, None] * freqs[None, :]
    cos = jnp.cos(ang)
    sin = jnp.sin(ang)
    cos2 = jnp.tile(jnp.concatenate([cos, cos], axis=1), (1, N_HEADS))
    sin2 = jnp.tile(jnp.concatenate([-sin, sin], axis=1), (1, N_HEADS))
    return cos2, sin2


def _pair_pad(w, axis):
    z = jnp.zeros_like(w)
    even = jnp.concatenate([w, z], axis=axis)
    odd = jnp.concatenate([z, w], axis=axis)
    sel = (jnp.arange(w.shape[0]) % 2 == 0).reshape(-1, 1, 1)
    return jnp.where(sel, even, odd)


def kernel(x_prompt, x_sample, cache_kv, cache_pe, page_table, c_prompt, c_sample, w_ada, b_ada, g_ffn1, w1_gate, w1_up, w1_down, g_mix, w_in, g_q, w_uq, g_kv, w_uk, w_uv, ln_v_g, ln_v_b, w_s, b_s, w_pa, w_pb, w_o, g_ffn2, w2_gate, w2_up, w2_down, g_final):
    B, S, D = x_prompt.shape
    DB, T, _ = x_sample.shape
    depth = w_ada.shape[0]
    assert depth == 1
    l = 0
    R = DB * T

    row = lambda v: v.reshape(1, -1)
    wi = w_in[l]
    seg = lambda a, n: wi[:, a:a + n]
    o_ckv = Q_LORA
    o_kpe = o_ckv + KV_LORA
    o_u = o_kpe + ROPE_DIM
    o_v = o_u + GV
    o_ga = o_v + GV
    o_gb = o_ga + D_MODEL
    w_in_r = jnp.concatenate([
        seg(0, Q_LORA), seg(o_ckv, KV_LORA), jnp.tile(seg(o_kpe, ROPE_DIM), (1, N_HEADS)),
        seg(o_u, GV), seg(o_v, GV), seg(o_ga, D_MODEL), seg(o_gb, D_MODEL)], axis=1).astype(BF16)
    wq = w_uq[l].reshape(Q_LORA, N_HEADS, NOPE_DIM + ROPE_DIM)
    w_uq_r = jnp.concatenate([wq[:, :, :NOPE_DIM].reshape(Q_LORA, -1),
                              wq[:, :, NOPE_DIM:].reshape(Q_LORA, -1)], axis=1).astype(BF16)
    w_uk_p = _pair_pad(jnp.transpose(w_uk[l], (1, 2, 0)), 1).astype(BF16)
    w_uv_p = _pair_pad(jnp.transpose(w_uv[l], (1, 0, 2)), 2).astype(BF16)
    bf = lambda w: w[l].astype(BF16)
    w1g, w1u, w1d = bf(w1_gate), bf(w1_up), bf(w1_down)
    w2g, w2u, w2d = bf(w2_gate), bf(w2_up), bf(w2_down)
    wpa, wpb, wo = bf(w_pa), bf(w_pb), bf(w_o)
    gf = row(g_final)

    w_mix_p = w_s[l]
    b_mix_p = jnp.repeat(b_s[l].T, GROUP_DIM, axis=1)
    reps = CHUNK // T
    w_mix_s = jnp.tile(w_s[l][:, :T, :T], (1, reps, reps))
    b_mix_s = jnp.repeat(jnp.tile(b_s[l][:, :T].T, (reps, 1)), GROUP_DIM, axis=1)

    mod = _ada(jnp.concatenate([c_prompt, c_sample], axis=0), w_ada[l], b_ada[l])
    mod_p = mod[:B].reshape(B, N_SUB, 3, D)
    mod_s = jnp.transpose(jnp.repeat(mod[B:], T, axis=0).reshape(R, N_SUB, 3, D), (1, 2, 0, 3))

    past_len = page_table.shape[1] * PAGE_SIZE
    cos_p, sin_p = _rope_tables(jnp.arange(S))
    cos_s, sin_s = _rope_tables(past_len + jnp.arange(R) % T)

    def layer(x, mods, per_row, tm, chunk_len, cos2, sin2, w_mix, b_mix, want_gv, attend):
        x1 = _ffn(x, mods(0), row(g_ffn1[l]), w1g, w1u, w1d, gf, per_row=per_row, final_norm=False, tm=tm)
        outs = _mixin(x1, mods(1), row(g_mix[l]), w_in_r, row(g_q[l]), w_uq_r, w_uk_p, row(g_kv[l]),
                      cos2, sin2, row(ln_v_g[l]), row(ln_v_b[l]), w_mix, b_mix, wpb,
                      per_row=per_row, chunk_len=chunk_len, want_gv=want_gv, tm=tm)
        kv, pe, kcat, kv_t, q_lat_t, q_pe_t, sa, mb = outs[:8]
        o_lat = attend(q_lat_t, q_pe_t, kcat, kv_t, kv, pe)
        x2 = _merge(x1, o_lat, sa, mb, mods(1), w_uv_p, wpa, wo, per_row=per_row, tm=tm)
        y = _ffn(x2, mods(2), row(g_ffn2[l]), w2g, w2u, w2d, gf, per_row=per_row, final_norm=True, tm=tm)
        return y, kv, pe, (outs[8] if want_gv else None)

    def attend_p(q_lat_t, q_pe_t, kcat, kv_t, kv, pe):
        return _pattn(q_lat_t, q_pe_t, kcat, kv_t)

    y_p, kv_p, pe_p, _ = layer(x_prompt, lambda i: mod_p[:, i], False, 512, CHUNK, cos_p, sin_p,
                               w_mix_p, b_mix_p, False, attend_p)

    cache_pe_t = jnp.swapaxes(cache_pe[l], 1, 2)

    def attend_s(q_lat_t, q_pe_t, kcat, kv_t, kv, pe):
        ql = jnp.transpose(q_lat_t.reshape(N_HEADS, KV_LORA, DB, T), (2, 0, 3, 1)).reshape(DB, N_HEADS * T, KV_LORA)
        qp = jnp.transpose(q_pe_t.reshape(N_HEADS, ROPE_DIM, DB, T), (2, 0, 3, 1)).reshape(DB, N_HEADS * T, ROPE_DIM)
        o = _sattn(page_table, ql, qp, kv.reshape(DB, T, KV_LORA), pe.reshape(DB, T, ROPE_DIM),
                   cache_kv[l], cache_pe_t)
        o = jnp.transpose(o.reshape(DB, N_HEADS, T, KV_LORA), (1, 0, 2, 3)).reshape(1, N_HEADS, R, KV_LORA)
        return o.astype(BF16)

    y_s, kv_s, pe_s, gv_s = layer(x_sample.reshape(1, R, D), lambda i: mod_s[i], True, R, T, cos_s, sin_s,
                                  w_mix_s, b_mix_s, True, attend_s)

    return (y_p, y_s.reshape(DB, T, D),
            kv_p.reshape(1, B, S, KV_LORA), pe_p.reshape(1, B, S, ROPE_DIM),
            kv_s.reshape(1, DB, T, KV_LORA), pe_s.reshape(1, DB, T, ROPE_DIM),
            gv_s.reshape(1, DB, T, GV))
```

```python
import functools

import jax
import jax.numpy as jnp
from jax import lax
from jax.experimental import pallas as pl
from jax.experimental.pallas import tpu as pltpu

D_MODEL = 1024
N_HEADS = 8
Q_LORA = 256
KV_LORA = 256
NOPE_DIM = 64
ROPE_DIM = 32
V_DIM = 64
ROPE_THETA = 10000.0
SM_SCALE = (NOPE_DIM + ROPE_DIM) ** -0.5
Q_SCALE = SM_SCALE * 1.4426950408889634
CHUNK = 128
GV = 512
N_GROUPS = 8
GROUP_DIM = GV // N_GROUPS
D_FF = 2816
N_SUB = 3
EPS = 1e-6
PAGE_SIZE = 128
PE_WIDE = N_HEADS * ROPE_DIM
KEY_CHUNK = 256

VMEM_LIMIT_BYTES = 56 * 1024 * 1024
LANES = 128

BF16 = jnp.bfloat16
F32 = jnp.float32

_OFF_CQ = 0
_OFF_CKV = _OFF_CQ + Q_LORA
_OFF_KPE = _OFF_CKV + KV_LORA
_OFF_U = _OFF_KPE + PE_WIDE
_OFF_V = _OFF_U + GV
_OFF_GA = _OFF_V + GV
_OFF_GB = _OFF_GA + D_MODEL
_N_IN_R = _OFF_GB + D_MODEL


def _dot(a, b):
    return jnp.dot(a, b, preferred_element_type=F32)


def _dot_nt(a, b):
    return lax.dot_general(a, b, (((1,), (1,)), ((), ())), preferred_element_type=F32)


def _rms(x, g):
    return x * lax.rsqrt(jnp.mean(x * x, axis=-1, keepdims=True) + EPS) * g


def _row_expander(tm, n_seq, rows_per_seq):
    if not rows_per_seq:
        return None
    r = lax.broadcasted_iota(jnp.int32, (tm, n_seq), 0)
    s = lax.broadcasted_iota(jnp.int32, (tm, n_seq), 1)
    return (r // rows_per_seq == s).astype(F32)


def _mod_row(mod_ref, k, expand):
    if expand is None:
        return mod_ref[k:k + 1, :]
    return jnp.dot(expand, mod_ref[k], precision=lax.Precision.HIGHEST, preferred_element_type=F32)


def _const_spec(shape):
    nd = len(shape)
    return pl.BlockSpec(shape, lambda *_: (0,) * nd, pipeline_mode=pl.Buffered(1))


def _params(n_grid):
    return pltpu.CompilerParams(
        dimension_semantics=("arbitrary",) * n_grid,
        vmem_limit_bytes=VMEM_LIMIT_BYTES,
    )


def _ada_kernel(c_ref, w_ref, b_ref, o_ref):
    c = c_ref[...]
    a = (c * jax.nn.sigmoid(c)).astype(BF16)
    o_ref[...] = _dot(a, w_ref[...].astype(BF16)) + b_ref[...]


def _ada(c_all, w_ada, b_ada):
    rows, d = c_all.shape
    n = w_ada.shape[1]
    tn = 1024
    return pl.pallas_call(
        _ada_kernel,
        out_shape=jax.ShapeDtypeStruct((rows, n), F32),
        grid=(n // tn,),
        in_specs=[
            pl.BlockSpec((rows, d), lambda j: (0, 0)),
            pl.BlockSpec((d, tn), lambda j: (0, j)),
            pl.BlockSpec((1, tn), lambda j: (0, j)),
        ],
        out_specs=pl.BlockSpec((rows, tn), lambda j: (0, j)),
        compiler_params=_params(1),
        name="ada",
    )(c_all, w_ada, b_ada.reshape(1, n))


_FF_CHUNKS = ((0, 1024), (1024, 1024), (2048, 768))


def _merge_mixers(x, ol_ref, sa_ref, mb_ref, gate, wuv_ref, wpa_ref, wo_ref):
    parts = []
    for jp in range(N_HEADS // 2):
        a = _dot(ol_ref[2 * jp], wuv_ref[2 * jp]) + _dot(ol_ref[2 * jp + 1], wuv_ref[2 * jp + 1])
        parts.append(a.astype(BF16))
    o_a = jnp.concatenate(parts, axis=1)
    merged = sa_ref[...].astype(F32) * _dot(o_a, wpa_ref[...]) + mb_ref[...].astype(F32)
    return x + gate * _dot(merged.astype(BF16), wo_ref[...])


def _ffn_kernel(*refs, rows_per_seq, final_norm, merge):
    if merge:
        (x_ref, ol_ref, sa_ref, mb_ref, modm_ref, wuv_ref, wpa_ref, wo_ref,
         mod_ref, g_ref, wg_ref, wu_ref, wd_ref, gf_ref, o_ref) = refs
    else:
        x_ref, mod_ref, g_ref, wg_ref, wu_ref, wd_ref, gf_ref, o_ref = refs
    x = x_ref[...]
    expand = _row_expander(x.shape[0], mod_ref.shape[1], rows_per_seq)
    if merge:
        x = _merge_mixers(x, ol_ref, sa_ref, mb_ref, _mod_row(modm_ref, 2, expand), wuv_ref, wpa_ref, wo_ref)
    shift = _mod_row(mod_ref, 0, expand)
    scale = _mod_row(mod_ref, 1, expand)
    gate = _mod_row(mod_ref, 2, expand)
    nb = (_rms(x, g_ref[...]) * (1.0 + scale) + shift).astype(BF16)
    acc = jnp.zeros(x.shape, F32)
    for off, width in _FF_CHUNKS:
        hg = _dot(nb, wg_ref[:, off:off + width])
        hu = _dot(nb, wu_ref[:, off:off + width])
        h = (hg * jax.nn.sigmoid(hg) * hu).astype(BF16)
        acc = acc + _dot(h, wd_ref[off:off + width, :])
    y = x + 0.5 * gate * acc
    if final_norm:
        y = _rms(y, gf_ref[...])
    o_ref[...] = y


def _mod_spec(mod, rows_per_seq):
    if rows_per_seq:
        return pl.BlockSpec(mod.shape, lambda g, i: (0, 0, 0))
    return pl.BlockSpec((None, 3, D_MODEL), lambda g, i: (g, 0, 0))


def _row_spec(tm, width):
    return pl.BlockSpec((None, tm, width), lambda g, i: (g, i, 0))


def _check_tile(x, tm, rows_per_seq):
    G, T, _ = x.shape
    assert T % tm == 0
    if rows_per_seq:
        assert G == 1 and tm == T, "per-sequence modulation is expanded for one tile holding every row"


def _ffn(x, mod, g, wg, wu, wd, g_final, *, rows_per_seq, final_norm, tm, merge_args=None):
    G, T, _ = x.shape
    _check_tile(x, tm, rows_per_seq)
    operands = [x]
    in_specs = [_row_spec(tm, D_MODEL)]
    if merge_args is not None:
        o_lat, sa, mb, mod_mix, w_uv_p, w_pa, w_o = merge_args
        operands += [o_lat, sa, mb, mod_mix, w_uv_p, w_pa, w_o]
        in_specs += [
            pl.BlockSpec((None, N_HEADS, tm, KV_LORA), lambda g, i: (g, 0, i, 0)),
            _row_spec(tm, D_MODEL),
            _row_spec(tm, D_MODEL),
            _mod_spec(mod_mix, rows_per_seq),
            _const_spec(w_uv_p.shape),
            _const_spec(w_pa.shape),
            _const_spec(w_o.shape),
        ]
    operands += [mod, g, wg, wu, wd, g_final]
    in_specs += [
        _mod_spec(mod, rows_per_seq),
        _const_spec((1, D_MODEL)),
        _const_spec(wg.shape),
        _const_spec(wu.shape),
        _const_spec(wd.shape),
        _const_spec((1, D_MODEL)),
    ]
    return pl.pallas_call(
        functools.partial(_ffn_kernel, rows_per_seq=rows_per_seq, final_norm=final_norm,
                          merge=merge_args is not None),
        out_shape=jax.ShapeDtypeStruct(x.shape, F32),
        grid=(G, T // tm),
        in_specs=in_specs,
        out_specs=_row_spec(tm, D_MODEL),
        compiler_params=_params(2),
        name="merge_ffn" if merge_args is not None else "ffn",
    )(*operands)


def _rope(x, cos2, sin2):
    halves = []
    for k in range(x.shape[1] // LANES):
        xs = x[:, k * LANES:(k + 1) * LANES]
        lane = lax.broadcasted_iota(jnp.int32, xs.shape, 1)
        first = (lane % ROPE_DIM) < (ROPE_DIM // 2)
        swapped = jnp.where(first, pltpu.roll(xs, LANES - ROPE_DIM // 2, 1), pltpu.roll(xs, ROPE_DIM // 2, 1))
        halves.append(swapped)
    swapped = jnp.concatenate(halves, axis=1)
    return x * cos2 + swapped * sin2


def _mixin_kernel(x_ref, mod_ref, g_ref, win_ref, gq_ref, wuq_ref, wuk_ref, gkv_ref, cos_ref, sin_ref,
                  lng_ref, lnb_ref, wmix_ref, bmix_ref, wpb_ref,
                  kv_ref, pe_ref, kcat_ref, kvt_ref, qlat_ref, qpe_ref, sa_ref, mb_ref, gv_ref,
                  *, rows_per_seq, chunk_len):
    x = x_ref[...]
    tm = x.shape[0]
    expand = _row_expander(tm, mod_ref.shape[1], rows_per_seq)
    shift = _mod_row(mod_ref, 0, expand)
    scale = _mod_row(mod_ref, 1, expand)
    nb = (_rms(x, g_ref[...]) * (1.0 + scale) + shift).astype(BF16)
    cos2 = cos_ref[...]
    sin2 = sin_ref[...]

    def proj(off, width):
        return _dot(nb, win_ref[:, off:off + width])

    cqn = _rms(proj(_OFF_CQ, Q_LORA), gq_ref[...]).astype(BF16)
    q = _dot(cqn, wuq_ref[...])
    for h in range(N_HEADS):
        pair = q[:, (h // 2) * LANES:(h // 2 + 1) * LANES].astype(BF16)
        qlat_h = _dot(pair, wuk_ref[h]) * Q_SCALE
        qlat_ref[h] = qlat_h.T.astype(BF16)
    qpe_ref[...] = (_rope(q[:, N_HEADS * NOPE_DIM:], cos2, sin2) * Q_SCALE).T.astype(BF16)

    kv = _rms(proj(_OFF_CKV, KV_LORA), gkv_ref[...])
    kv_ref[...] = kv
    kpe = _rope(proj(_OFF_KPE, PE_WIDE), cos2, sin2)
    pe_ref[...] = kpe[:, :ROPE_DIM]
    kcat_ref[...] = jnp.concatenate([kv.astype(BF16), kpe.astype(BF16)], axis=1)
    kv_t = kv.T.astype(BF16)
    for c in range(tm // KEY_CHUNK):
        kvt_ref[c] = kv_t[:, c * KEY_CHUNK:(c + 1) * KEY_CHUNK]

    u = jax.nn.gelu(proj(_OFF_U, GV))
    v = jax.nn.gelu(proj(_OFF_V, GV))
    mu = jnp.mean(v, axis=-1, keepdims=True)
    vc = v - mu
    var = jnp.mean(vc * vc, axis=-1, keepdims=True)
    v_n = vc * lax.rsqrt(var + EPS) * lng_ref[...] + lnb_ref[...]
    if gv_ref is not None:
        gv_ref[...] = v_n
    vnb = v_n.astype(BF16)
    row = lax.broadcasted_iota(jnp.int32, (CHUNK, CHUNK), 0)
    col = lax.broadcasted_iota(jnp.int32, (CHUNK, CHUNK), 1)
    keep = (col <= row) & ((row // chunk_len) == (col // chunk_len))
    wm = [jnp.where(keep, wmix_ref[g], 0.0).astype(BF16) for g in range(N_GROUPS)]
    lane = lax.broadcasted_iota(jnp.int32, (CHUNK, LANES), 1)
    low = lane < GROUP_DIM
    bias = bmix_ref[...]
    tiles = []
    for c in range(tm // CHUNK):
        cols = []
        for jp in range(GV // LANES):
            vv = vnb[c * CHUNK:(c + 1) * CHUNK, jp * LANES:(jp + 1) * LANES]
            cols.append(jnp.where(low, _dot(wm[2 * jp], vv), _dot(wm[2 * jp + 1], vv)))
        tiles.append(jnp.concatenate(cols, axis=1) + bias)
    mixed = jnp.concatenate(tiles, axis=0)
    o_b = (u * mixed).astype(BF16)
    mb_ref[...] = (jax.nn.sigmoid(proj(_OFF_GB, D_MODEL)) * _dot(o_b, wpb_ref[...])).astype(BF16)
    sa_ref[...] = jax.nn.sigmoid(proj(_OFF_GA, D_MODEL)).astype(BF16)


def _mixin(x, mod, g_mix, w_in_r, g_q, w_uq_r, w_uk_p, g_kv, cos2, sin2, ln_g, ln_b, w_mix, b_mix, w_pb,
           *, rows_per_seq, chunk_len, want_gv, tm):
    G, T, _ = x.shape
    _check_tile(x, tm, rows_per_seq)
    grid = (G, T // tm)
    out_shape = [
        jax.ShapeDtypeStruct((G, T, KV_LORA), F32),
        jax.ShapeDtypeStruct((G, T, ROPE_DIM), F32),
        jax.ShapeDtypeStruct((G, T, KV_LORA + PE_WIDE), BF16),
        jax.ShapeDtypeStruct((G, T // KEY_CHUNK, KV_LORA, KEY_CHUNK), BF16),
        jax.ShapeDtypeStruct((G, N_HEADS, KV_LORA, T), BF16),
        jax.ShapeDtypeStruct((G, PE_WIDE, T), BF16),
        jax.ShapeDtypeStruct((G, T, D_MODEL), BF16),
        jax.ShapeDtypeStruct((G, T, D_MODEL), BF16),
    ]
    out_specs = [
        _row_spec(tm, KV_LORA),
        _row_spec(tm, ROPE_DIM),
        _row_spec(tm, KV_LORA + PE_WIDE),
        pl.BlockSpec((None, tm // KEY_CHUNK, KV_LORA, KEY_CHUNK), lambda g, i: (g, i, 0, 0)),
        pl.BlockSpec((None, N_HEADS, KV_LORA, tm), lambda g, i: (g, 0, 0, i)),
        pl.BlockSpec((None, PE_WIDE, tm), lambda g, i: (g, 0, i)),
        _row_spec(tm, D_MODEL),
        _row_spec(tm, D_MODEL),
    ]
    if want_gv:
        out_shape.append(jax.ShapeDtypeStruct((G, T, GV), F32))
        out_specs.append(_row_spec(tm, GV))
        body = functools.partial(_mixin_kernel, rows_per_seq=rows_per_seq, chunk_len=chunk_len)
    else:
        def body(*refs):
            _mixin_kernel(*refs, None, rows_per_seq=rows_per_seq, chunk_len=chunk_len)
    return pl.pallas_call(
        body,
        out_shape=out_shape,
        grid=grid,
        in_specs=[
            _row_spec(tm, D_MODEL),
            _mod_spec(mod, rows_per_seq),
            _const_spec((1, D_MODEL)),
            _const_spec(w_in_r.shape),
            _const_spec((1, Q_LORA)),
            _const_spec(w_uq_r.shape),
            _const_spec(w_uk_p.shape),
            _const_spec((1, KV_LORA)),
            pl.BlockSpec((tm, PE_WIDE), lambda g, i: (i, 0)),
            pl.BlockSpec((tm, PE_WIDE), lambda g, i: (i, 0)),
            _const_spec((1, GV)),
            _const_spec((1, GV)),
            _const_spec(w_mix.shape),
            _const_spec(b_mix.shape),
            _const_spec(w_pb.shape),
        ],
        out_specs=out_specs,
        compiler_params=_params(2),
        name="mixin",
    )(x, mod, g_mix, w_in_r, g_q, w_uq_r, w_uk_p, g_kv, cos2, sin2, ln_g, ln_b, w_mix, b_mix, w_pb)


_TQ = 128
_HEADS_PER_BLOCK = 2
_COL_BLOCK = _HEADS_PER_BLOCK * _TQ


def _pattn_kernel(qlat_ref, qpe_ref, kcat_ref, kvt_ref, o_ref, qt_sc, s0_sc, s1_sc, m_sc, l_sc, acc_sc):
    i = pl.program_id(1)
    tk = KEY_CHUNK
    head_of_row = lax.broadcasted_iota(jnp.int32, (PE_WIDE, _TQ), 0) // ROPE_DIM
    qp = qpe_ref[...]
    for h in range(N_HEADS):
        qt_sc[0:KV_LORA, h * _TQ:(h + 1) * _TQ] = qlat_ref[h]
        qt_sc[KV_LORA:, h * _TQ:(h + 1) * _TQ] = jnp.where(head_of_row == h, qp, jnp.zeros_like(qp))
    m_sc[...] = jnp.full(m_sc.shape, -1e30, F32)
    l_sc[...] = jnp.zeros(l_sc.shape, F32)
    acc_sc[...] = jnp.zeros(acc_sc.shape, F32)

    def scores(j, s_ref):
        k = kcat_ref[pl.ds(pl.multiple_of(j * tk, tk), tk), :]
        s_ref[...] = _dot(k, qt_sc[...])

    def accumulate(j, s_ref, masked):
        v_t = kvt_ref[j]
        for cb in range(N_HEADS // _HEADS_PER_BLOCK):
            cols = slice(cb * _COL_BLOCK, (cb + 1) * _COL_BLOCK)
            s = s_ref[:, cols]
            if masked:
                q_pos = i * _TQ + lax.broadcasted_iota(jnp.int32, s.shape, 1) % _TQ
                k_pos = j * tk + lax.broadcasted_iota(jnp.int32, s.shape, 0)
                s = jnp.where(k_pos <= q_pos, s, -jnp.inf)
            m_prev = m_sc[:, cols]
            m_new = jnp.maximum(m_prev, jnp.max(s, axis=0, keepdims=True))
            alpha = jnp.exp2(m_prev - m_new)
            p = jnp.exp2(s - m_new)
            l_sc[:, cols] = alpha * l_sc[:, cols] + jnp.sum(p, axis=0, keepdims=True)
            acc_sc[:, cols] = alpha * acc_sc[:, cols] + _dot(v_t, p.astype(BF16))
            m_sc[:, cols] = m_new

    n_full = (i * _TQ) // tk
    scores(0, s0_sc)

    def pair(p, carry):
        j = 2 * p
        scores(j + 1, s1_sc)
        accumulate(j, s0_sc, False)
        scores(j + 2, s0_sc)
        accumulate(j + 1, s1_sc, False)
        return carry

    lax.fori_loop(0, n_full // 2, pair, 0)

    @pl.when(n_full % 2 == 0)
    def _():
        accumulate(n_full, s0_sc, True)

    @pl.when(n_full % 2 == 1)
    def _():
        scores(n_full, s1_sc)
        accumulate(n_full - 1, s0_sc, False)
        accumulate(n_full, s1_sc, True)

    o_t = acc_sc[...] / l_sc[...]
    for h in range(N_HEADS):
        o_ref[h] = o_t[:, h * _TQ:(h + 1) * _TQ].T.astype(BF16)


def _pattn(q_lat_t, q_pe_t, kcat, kv_t):
    B, H, C, S = q_lat_t.shape
    cols = H * _TQ
    return pl.pallas_call(
        _pattn_kernel,
        out_shape=jax.ShapeDtypeStruct((B, H, S, C), BF16),
        grid=(B, S // _TQ),
        in_specs=[
            pl.BlockSpec((None, H, C, _TQ), lambda b, i: (b, 0, 0, i)),
            pl.BlockSpec((None, PE_WIDE, _TQ), lambda b, i: (b, 0, i)),
            pl.BlockSpec((None, S, C + PE_WIDE), lambda b, i: (b, 0, 0)),
            pl.BlockSpec((None, S // KEY_CHUNK, C, KEY_CHUNK), lambda b, i: (b, 0, 0, 0)),
        ],
        out_specs=pl.BlockSpec((None, H, _TQ, C), lambda b, i: (b, 0, i, 0)),
        scratch_shapes=[
            pltpu.VMEM((C + PE_WIDE, cols), BF16),
            pltpu.VMEM((KEY_CHUNK, cols), F32),
            pltpu.VMEM((KEY_CHUNK, cols), F32),
            pltpu.VMEM((1, cols), F32),
            pltpu.VMEM((1, cols), F32),
            pltpu.VMEM((C, cols), F32),
        ],
        compiler_params=_params(2),
        name="pattn",
    )(q_lat_t, q_pe_t, kcat, kv_t)


_PAGES_PER_CHUNK = 32


_DMA_SLOTS = 3


def _sattn_kernel(pt_ref, ql_ref, qp_ref, kvn_ref, pen_ref, ckv_hbm, cpet_hbm, o_ref,
                  kvbuf, pebuf, kvb16_0, kvb16_1, s0_sc, s1_sc, m_sc, l_sc, acc_sc, sem, *, n_pages, t_new):
    pc = _PAGES_PER_CHUNK
    n_chunks = n_pages // pc
    n_seq, rows, _ = ql_ref.shape
    total = n_seq * n_chunks
    assert n_chunks % 2 == 0 and total > _DMA_SLOTS
    work = ((kvb16_0, s0_sc), (kvb16_1, s1_sc))

    def copies(g, slot):
        b = g // n_chunks
        c = g % n_chunks
        out = []
        for p in range(pc):
            page = pt_ref[b, c * pc + p]
            dst = pl.ds(p * PAGE_SIZE, PAGE_SIZE)
            out.append(pltpu.make_async_copy(ckv_hbm.at[page], kvbuf.at[slot, dst], sem.at[0, slot]))
            out.append(pltpu.make_async_copy(cpet_hbm.at[page], pebuf.at[slot, :, dst], sem.at[1, slot]))
        return out

    def start(g):
        for cp in copies(g, g % _DMA_SLOTS):
            cp.start()

    def wait(g):
        for cp in copies(g, g % _DMA_SLOTS):
            cp.wait()

    def scores(g, w):
        kvb16, s_sc = work[w]
        b = g // n_chunks
        slot = g % _DMA_SLOTS
        wait(g)
        kvb = kvbuf[slot].astype(BF16)
        kvb16[...] = kvb
        peb = pebuf[slot].astype(BF16)
        s_sc[...] = _dot_nt(ql_ref[b], kvb) + _dot(qp_ref[b], peb)

    def reset_state():
        m_sc[...] = jnp.full(m_sc.shape, -1e30, F32)
        l_sc[...] = jnp.zeros(l_sc.shape, F32)
        acc_sc[...] = jnp.zeros(acc_sc.shape, F32)

    def accumulate(w):
        kvb16, s_sc = work[w]
        s = s_sc[...]
        m_prev = m_sc[...]
        m_new = jnp.maximum(m_prev, jnp.max(s, axis=-1, keepdims=True))
        alpha = jnp.exp2(m_prev - m_new)
        p = jnp.exp2(s - m_new)
        l_sc[...] = alpha * l_sc[...] + jnp.sum(p, axis=-1, keepdims=True)
        acc_sc[...] = alpha * acc_sc[...] + _dot(p.astype(BF16), kvb16[...])
        m_sc[...] = m_new

    def finish(b):
        qlf = ql_ref[b].astype(F32)
        qpf = qp_ref[b].astype(F32)
        kvn = kvn_ref[b]
        pen = pen_ref[b]
        t_of_row = lax.broadcasted_iota(jnp.int32, (rows, 1), 0) % t_new
        s_new = []
        for t in range(t_new):
            st = (jnp.sum(qlf * kvn[t:t + 1, :], axis=-1, keepdims=True)
                  + jnp.sum(qpf * pen[t:t + 1, :], axis=-1, keepdims=True))
            s_new.append(jnp.where(t <= t_of_row, st, -jnp.inf))
        m_prev = m_sc[...]
        m_new = m_prev
        for st in s_new:
            m_new = jnp.maximum(m_new, st)
        alpha = jnp.exp2(m_prev - m_new)
        l_new = alpha * l_sc[...]
        acc = alpha * acc_sc[...]
        for t, st in enumerate(s_new):
            pt = jnp.exp2(st - m_new)
            l_new = l_new + pt
            acc = acc + pt * kvn[t:t + 1, :]
        o_ref[b] = acc / l_new
        reset_state()

    reset_state()
    for g0 in range(_DMA_SLOTS):
        start(g0)
    scores(0, 0)

    def body(k, carry):
        g = 2 * k + 1
        start(g + _DMA_SLOTS - 1)
        scores(g, 1)
        accumulate(0)
        start(g + _DMA_SLOTS)
        scores(g + 1, 0)
        accumulate(1)

        @pl.when((g + 1) % n_chunks == 0)
        def _():
            finish((g + 1) // n_chunks - 1)

        return carry

    lax.fori_loop(0, total // 2 - 1 + pl.program_id(0), body, 0)
    scores(total - 1, 1)
    accumulate(0)
    accumulate(1)
    finish(n_seq - 1)
    wait(total)


def _sattn(page_table, q_lat, q_pe, kv_new, pe_new, cache_kv, cache_pe_t):
    DB, rows, C = q_lat.shape
    t_new = kv_new.shape[1]
    n_pages = page_table.shape[1]
    keys = _PAGES_PER_CHUNK * PAGE_SIZE
    page_table = jnp.concatenate([page_table, page_table[:1]], axis=0)
    full = lambda shape: pl.BlockSpec(shape, lambda i, pt: (0,) * len(shape))
    grid_spec = pltpu.PrefetchScalarGridSpec(
        num_scalar_prefetch=1,
        grid=(1,),
        in_specs=[
            full(q_lat.shape),
            full(q_pe.shape),
            full(kv_new.shape),
            full(pe_new.shape),
            pl.BlockSpec(memory_space=pl.ANY),
            pl.BlockSpec(memory_space=pl.ANY),
        ],
        out_specs=full((DB, rows, C)),
        scratch_shapes=[
            pltpu.VMEM((_DMA_SLOTS, keys, C), F32),
            pltpu.VMEM((_DMA_SLOTS, ROPE_DIM, keys), F32),
            pltpu.VMEM((keys, C), BF16),
            pltpu.VMEM((keys, C), BF16),
            pltpu.VMEM((rows, keys), F32),
            pltpu.VMEM((rows, keys), F32),
            pltpu.VMEM((rows, 1), F32),
            pltpu.VMEM((rows, 1), F32),
            pltpu.VMEM((rows, C), F32),
            pltpu.SemaphoreType.DMA((2, _DMA_SLOTS)),
        ],
    )
    return pl.pallas_call(
        functools.partial(_sattn_kernel, n_pages=n_pages, t_new=t_new),
        out_shape=jax.ShapeDtypeStruct((DB, rows, C), F32),
        grid_spec=grid_spec,
        compiler_params=_params(1),
        name="sattn",
    )(page_table, q_lat, q_pe, kv_new, pe_new, cache_kv, cache_pe_t)


def _rope_tables(pos):
    half = ROPE_DIM // 2
    freqs = ROPE_THETA ** (-2.0 * jnp.arange(half, dtype=F32) / ROPE_DIM)
    ang = pos.astype(F32)[:, None] * freqs[None, :]
    cos = jnp.cos(ang)
    sin = jnp.sin(ang)
    cos2 = jnp.tile(jnp.concatenate([cos, cos], axis=1), (1, N_HEADS))
    sin2 = jnp.tile(jnp.concatenate([-sin, sin], axis=1), (1, N_HEADS))
    return cos2, sin2


def _pair_pad(w, axis):
    z = jnp.zeros_like(w)
    even = jnp.concatenate([w, z], axis=axis)
    odd = jnp.concatenate([z, w], axis=axis)
    sel = (jnp.arange(w.shape[0]) % 2 == 0).reshape(-1, 1, 1)
    return jnp.where(sel, even, odd)


def kernel(x_prompt, x_sample, cache_kv, cache_pe, page_table, c_prompt, c_sample, w_ada, b_ada, g_ffn1, w1_gate, w1_up, w1_down, g_mix, w_in, g_q, w_uq, g_kv, w_uk, w_uv, ln_v_g, ln_v_b, w_s, b_s, w_pa, w_pb, w_o, g_ffn2, w2_gate, w2_up, w2_down, g_final):
    B, S, D = x_prompt.shape
    DB, T, _ = x_sample.shape
    depth = w_ada.shape[0]
    assert depth == 1
    l = 0
    R = DB * T

    row = lambda v: v.reshape(1, -1)
    wi = w_in[l]
    seg = lambda a, n: wi[:, a:a + n]
    o_ckv = Q_LORA
    o_kpe = o_ckv + KV_LORA
    o_u = o_kpe + ROPE_DIM
    o_v = o_u + GV
    o_ga = o_v + GV
    o_gb = o_ga + D_MODEL
    w_in_r = jnp.concatenate([
        seg(0, Q_LORA), seg(o_ckv, KV_LORA), jnp.tile(seg(o_kpe, ROPE_DIM), (1, N_HEADS)),
        seg(o_u, GV), seg(o_v, GV), seg(o_ga, D_MODEL), seg(o_gb, D_MODEL)], axis=1).astype(BF16)
    wq = w_uq[l].reshape(Q_LORA, N_HEADS, NOPE_DIM + ROPE_DIM)
    w_uq_r = jnp.concatenate([wq[:, :, :NOPE_DIM].reshape(Q_LORA, -1),
                              wq[:, :, NOPE_DIM:].reshape(Q_LORA, -1)], axis=1).astype(BF16)
    w_uk_p = _pair_pad(jnp.transpose(w_uk[l], (1, 2, 0)), 1).astype(BF16)
    w_uv_p = _pair_pad(jnp.transpose(w_uv[l], (1, 0, 2)), 2).astype(BF16)
    bf = lambda w: w[l].astype(BF16)
    w1g, w1u, w1d = bf(w1_gate), bf(w1_up), bf(w1_down)
    w2g, w2u, w2d = bf(w2_gate), bf(w2_up), bf(w2_down)
    wpa, wpb, wo = bf(w_pa), bf(w_pb), bf(w_o)
    gf = row(g_final)

    w_mix_p = w_s[l]
    b_mix_p = jnp.repeat(b_s[l].T, GROUP_DIM, axis=1)
    reps = CHUNK // T
    w_mix_s = jnp.tile(w_s[l][:, :T, :T], (1, reps, reps))
    b_mix_s = jnp.repeat(jnp.tile(b_s[l][:, :T].T, (reps, 1)), GROUP_DIM, axis=1)

    mod = _ada(jnp.concatenate([c_prompt, c_sample], axis=0), w_ada[l], b_ada[l])
    mod_p = mod[:B].reshape(B, N_SUB, 3, D)
    mod_s = jnp.transpose(mod[B:].reshape(DB, N_SUB, 3, D), (1, 2, 0, 3))

    past_len = page_table.shape[1] * PAGE_SIZE
    cos_p, sin_p = _rope_tables(jnp.arange(S))
    cos_s, sin_s = _rope_tables(past_len + jnp.arange(R) % T)

    def layer(x, mods, rows_per_seq, tm, chunk_len, cos2, sin2, w_mix, b_mix, want_gv, attend):
        x1 = _ffn(x, mods(0), row(g_ffn1[l]), w1g, w1u, w1d, gf,
                  rows_per_seq=rows_per_seq, final_norm=False, tm=tm)
        outs = _mixin(x1, mods(1), row(g_mix[l]), w_in_r, row(g_q[l]), w_uq_r, w_uk_p, row(g_kv[l]),
                      cos2, sin2, row(ln_v_g[l]), row(ln_v_b[l]), w_mix, b_mix, wpb,
                      rows_per_seq=rows_per_seq, chunk_len=chunk_len, want_gv=want_gv, tm=tm)
        kv, pe, kcat, kv_t, q_lat_t, q_pe_t, sa, mb = outs[:8]
        o_lat = attend(q_lat_t, q_pe_t, kcat, kv_t, kv, pe)
        y = _ffn(x1, mods(2), row(g_ffn2[l]), w2g, w2u, w2d, gf,
                 rows_per_seq=rows_per_seq, final_norm=True, tm=tm,
                 merge_args=(o_lat, sa, mb, mods(1), w_uv_p, wpa, wo))
        return y, kv, pe, (outs[8] if want_gv else None)

    def attend_p(q_lat_t, q_pe_t, kcat, kv_t, kv, pe):
        return _pattn(q_lat_t, q_pe_t, kcat, kv_t)

    y_p, kv_p, pe_p, _ = layer(x_prompt, lambda i: mod_p[:, i], 0, 512, CHUNK, cos_p, sin_p,
                               w_mix_p, b_mix_p, False, attend_p)

    cache_pe_t = jnp.swapaxes(cache_pe[l], 1, 2)

    def attend_s(q_lat_t, q_pe_t, kcat, kv_t, kv, pe):
        ql = jnp.transpose(q_lat_t.reshape(N_HEADS, KV_LORA, DB, T), (2, 0, 3, 1)).reshape(DB, N_HEADS * T, KV_LORA)
        qp = jnp.transpose(q_pe_t.reshape(N_HEADS, ROPE_DIM, DB, T), (2, 0, 3, 1)).reshape(DB, N_HEADS * T, ROPE_DIM)
        o = _sattn(page_table, ql, qp, kv.reshape(DB, T, KV_LORA), pe.reshape(DB, T, ROPE_DIM),
                   cache_kv[l], cache_pe_t)
        o = jnp.transpose(o.reshape(DB, N_HEADS, T, KV_LORA), (1, 0, 2, 3)).reshape(1, N_HEADS, R, KV_LORA)
        return o.astype(BF16)

    y_s, kv_s, pe_s, gv_s = layer(x_sample.reshape(1, R, D), lambda i: mod_s[i], T, R, T, cos_s, sin_s,
                                  w_mix_s, b_mix_s, True, attend_s)

    return (y_p, y_s.reshape(DB, T, D),
            kv_p.reshape(1, B, S, KV_LORA), pe_p.reshape(1, B, S, ROPE_DIM),
            kv_s.reshape(1, DB, T, KV_LORA), pe_s.reshape(1, DB, T, ROPE_DIM),
            gv_s.reshape(1, DB, T, GV))
```

```python
import functools

import jax
import jax.numpy as jnp
from jax import lax
from jax.experimental import pallas as pl
from jax.experimental.pallas import tpu as pltpu

D_MODEL = 1024
N_HEADS = 8
Q_LORA = 256
KV_LORA = 256
NOPE_DIM = 64
ROPE_DIM = 32
V_DIM = 64
ROPE_THETA = 10000.0
SM_SCALE = (NOPE_DIM + ROPE_DIM) ** -0.5
Q_SCALE = SM_SCALE * 1.4426950408889634
CHUNK = 128
GV = 512
N_GROUPS = 8
GROUP_DIM = GV // N_GROUPS
D_FF = 2816
N_SUB = 3
EPS = 1e-6
PAGE_SIZE = 128
PE_WIDE = N_HEADS * ROPE_DIM
KEY_CHUNK = 256

VMEM_LIMIT_BYTES = 56 * 1024 * 1024
LANES = 128

BF16 = jnp.bfloat16
F32 = jnp.float32

_OFF_CQ = 0
_OFF_CKV = _OFF_CQ + Q_LORA
_OFF_KPE = _OFF_CKV + KV_LORA
_OFF_U = _OFF_KPE + PE_WIDE
_OFF_V = _OFF_U + GV
_OFF_GA = _OFF_V + GV
_OFF_GB = _OFF_GA + D_MODEL
_N_IN_R = _OFF_GB + D_MODEL


def _dot(a, b):
    return jnp.dot(a, b, preferred_element_type=F32)


def _dot_nt(a, b):
    return lax.dot_general(a, b, (((1,), (1,)), ((), ())), preferred_element_type=F32)


def _rms(x, g):
    return x * lax.rsqrt(jnp.mean(x * x, axis=-1, keepdims=True) + EPS) * g


def _row_expander(tm, n_seq, rows_per_seq, row0=0):
    if not rows_per_seq:
        return None
    r = lax.broadcasted_iota(jnp.int32, (tm, n_seq), 0) + row0
    s = lax.broadcasted_iota(jnp.int32, (tm, n_seq), 1)
    return (r // rows_per_seq == s).astype(F32)


def _mod_row(mod_ref, k, expand):
    if expand is None:
        return mod_ref[k:k + 1, :]
    return jnp.dot(expand, mod_ref[k], precision=lax.Precision.HIGHEST, preferred_element_type=F32)


def _const_spec(shape):
    nd = len(shape)
    return pl.BlockSpec(shape, lambda *_: (0,) * nd, pipeline_mode=pl.Buffered(1))


def _params(n_grid, flags=None):
    return pltpu.CompilerParams(
        dimension_semantics=("arbitrary",) * n_grid,
        vmem_limit_bytes=VMEM_LIMIT_BYTES,
        flags=flags,
    )


def _ada_kernel(c_ref, w_ref, b_ref, o_ref):
    c = c_ref[...]
    a = (c * jax.nn.sigmoid(c)).astype(BF16)
    o_ref[...] = _dot(a, w_ref[...].astype(BF16)) + b_ref[...]


def _ada(c_all, w_ada, b_ada):
    rows, d = c_all.shape
    n = w_ada.shape[1]
    tn = 1024
    return pl.pallas_call(
        _ada_kernel,
        out_shape=jax.ShapeDtypeStruct((rows, n), F32),
        grid=(n // tn,),
        in_specs=[
            pl.BlockSpec((rows, d), lambda j: (0, 0)),
            pl.BlockSpec((d, tn), lambda j: (0, j)),
            pl.BlockSpec((1, tn), lambda j: (0, j)),
        ],
        out_specs=pl.BlockSpec((rows, tn), lambda j: (0, j)),
        compiler_params=_params(1),
        name="ada",
    )(c_all, w_ada, b_ada.reshape(1, n))


_FF_CHUNKS = ((0, 1536), (1536, 1280))


def _merge_mixers(x, ol_ref, sa_ref, mb_ref, gate, wuv_ref, wpa_ref, wo_ref):
    parts = []
    for jp in range(N_HEADS // 2):
        a = _dot(ol_ref[2 * jp], wuv_ref[2 * jp]) + _dot(ol_ref[2 * jp + 1], wuv_ref[2 * jp + 1])
        parts.append(a.astype(BF16))
    o_a = jnp.concatenate(parts, axis=1)
    merged = sa_ref[...].astype(F32) * _dot(o_a, wpa_ref[...]) + mb_ref[...].astype(F32)
    return x + gate * _dot(merged.astype(BF16), wo_ref[...])


def _ffn_stages(x_fn, mod_ref, g_ref, wg_ref, wu_ref, wd_ref, gf_ref, o_ref, expand, final_norm):
    st = {}

    def norm():
        x = x_fn()
        shift = _mod_row(mod_ref, 0, expand)
        scale = _mod_row(mod_ref, 1, expand)
        st["x"] = x
        st["gate"] = _mod_row(mod_ref, 2, expand)
        st["nb"] = (_rms(x, g_ref[...]) * (1.0 + scale) + shift).astype(BF16)

    def gate_proj(c):
        off, width = _FF_CHUNKS[c]
        st["hg"] = _dot(st["nb"], wg_ref[:, off:off + width])

    def up_proj(c):
        off, width = _FF_CHUNKS[c]
        hg = st.pop("hg")
        st["h"] = (hg * jax.nn.sigmoid(hg) * _dot(st["nb"], wu_ref[:, off:off + width])).astype(BF16)

    def down_proj(c):
        off, width = _FF_CHUNKS[c]
        d = _dot(st.pop("h"), wd_ref[off:off + width, :])
        st["acc"] = d if c == 0 else st["acc"] + d

    def final():
        y = st["x"] + 0.5 * st["gate"] * st["acc"]
        if final_norm:
            y = _rms(y, gf_ref[...])
        o_ref[...] = y

    stages = [norm]
    for c in range(len(_FF_CHUNKS)):
        stages += [functools.partial(gate_proj, c), functools.partial(up_proj, c), functools.partial(down_proj, c)]
    return stages + [final]


def _ffn_kernel(*refs, rows_per_seq, final_norm, merge):
    if merge:
        (x_ref, ol_ref, sa_ref, mb_ref, modm_ref, wuv_ref, wpa_ref, wo_ref,
         mod_ref, g_ref, wg_ref, wu_ref, wd_ref, gf_ref, o_ref) = refs
    else:
        x_ref, mod_ref, g_ref, wg_ref, wu_ref, wd_ref, gf_ref, o_ref = refs
    expand = _row_expander(x_ref.shape[0], mod_ref.shape[1], rows_per_seq)

    def x_fn():
        x = x_ref[...]
        if merge:
            x = _merge_mixers(x, ol_ref, sa_ref, mb_ref, _mod_row(modm_ref, 2, expand), wuv_ref, wpa_ref, wo_ref)
        return x

    for stage in _ffn_stages(x_fn, mod_ref, g_ref, wg_ref, wu_ref, wd_ref, gf_ref, o_ref, expand, final_norm):
        stage()


def _mod_spec(mod, rows_per_seq):
    if rows_per_seq:
        return pl.BlockSpec(mod.shape, lambda g, i, *_: (0, 0, 0))
    return pl.BlockSpec((None, 3, D_MODEL), lambda g, i, *_: (g, 0, 0))


def _row_spec(tm, width):
    return pl.BlockSpec((None, tm, width), lambda g, i, *_: (g, i, 0))


def _check_tile(x, tm, rows_per_seq):
    G, T, _ = x.shape
    assert T % tm == 0
    if rows_per_seq:
        assert G == 1 and tm == T, "per-sequence modulation is expanded for one tile holding every row"


def _ffn(x, mod, g, wg, wu, wd, g_final, *, rows_per_seq, final_norm, tm, merge_args=None):
    G, T, _ = x.shape
    _check_tile(x, tm, rows_per_seq)
    operands = [x]
    in_specs = [_row_spec(tm, D_MODEL)]
    if merge_args is not None:
        o_lat, sa, mb, mod_mix, w_uv_p, w_pa, w_o = merge_args
        operands += [o_lat, sa, mb, mod_mix, w_uv_p, w_pa, w_o]
        in_specs += [
            pl.BlockSpec((None, N_HEADS, tm, KV_LORA), lambda g, i: (g, 0, i, 0)),
            _row_spec(tm, D_MODEL),
            _row_spec(tm, D_MODEL),
            _mod_spec(mod_mix, rows_per_seq),
            _const_spec(w_uv_p.shape),
            _const_spec(w_pa.shape),
            _const_spec(w_o.shape),
        ]
    operands += [mod, g, wg, wu, wd, g_final]
    in_specs += [
        _mod_spec(mod, rows_per_seq),
        _const_spec((1, D_MODEL)),
        _const_spec(wg.shape),
        _const_spec(wu.shape),
        _const_spec(wd.shape),
        _const_spec((1, D_MODEL)),
    ]
    return pl.pallas_call(
        functools.partial(_ffn_kernel, rows_per_seq=rows_per_seq, final_norm=final_norm,
                          merge=merge_args is not None),
        out_shape=jax.ShapeDtypeStruct(x.shape, F32),
        grid=(G, T // tm),
        in_specs=in_specs,
        out_specs=_row_spec(tm, D_MODEL),
        compiler_params=_params(2),
        name="merge_ffn" if merge_args is not None else "ffn",
    )(*operands)


_MIXIN_SUB_ROWS = 512


def _rope(x, cos2, sin2):
    halves = []
    for k in range(x.shape[1] // LANES):
        xs = x[:, k * LANES:(k + 1) * LANES]
        lane = lax.broadcasted_iota(jnp.int32, xs.shape, 1)
        first = (lane % ROPE_DIM) < (ROPE_DIM // 2)
        swapped = jnp.where(first, pltpu.roll(xs, LANES - ROPE_DIM // 2, 1), pltpu.roll(xs, ROPE_DIM // 2, 1))
        halves.append(swapped)
    swapped = jnp.concatenate(halves, axis=1)
    return x * cos2 + swapped * sin2


def _mixin_kernel(x_ref, mod_ref, g_ref, win_ref, gq_ref, wuq_ref, wuk_ref, gkv_ref, cos_ref, sin_ref,
                  lng_ref, lnb_ref, wmix_ref, bmix_ref, wpb_ref,
                  kv_ref, pe_ref, kcat_ref, kvt_ref, qlat_ref, qpe_ref, sa_ref, mb_ref, gv_ref,
                  *, rows_per_seq, chunk_len):
    tm = x_ref.shape[0]
    row = lax.broadcasted_iota(jnp.int32, (CHUNK, CHUNK), 0)
    col = lax.broadcasted_iota(jnp.int32, (CHUNK, CHUNK), 1)
    keep = (col <= row) & ((row // chunk_len) == (col // chunk_len))
    wm = [jnp.where(keep, wmix_ref[g], 0.0).astype(BF16) for g in range(N_GROUPS)]
    lane = lax.broadcasted_iota(jnp.int32, (CHUNK, LANES), 1)
    low = lane < GROUP_DIM
    bias = bmix_ref[...]

    sub = _MIXIN_SUB_ROWS
    for r in range(tm // sub):
        rows = slice(r * sub, (r + 1) * sub)
        x = x_ref[rows, :]
        expand = _row_expander(sub, mod_ref.shape[1], rows_per_seq, r * sub)
        shift = _mod_row(mod_ref, 0, expand)
        scale = _mod_row(mod_ref, 1, expand)
        nb = (_rms(x, g_ref[...]) * (1.0 + scale) + shift).astype(BF16)
        cos2 = cos_ref[rows, :]
        sin2 = sin_ref[rows, :]

        def proj(off, width, nb=nb):
            return _dot(nb, win_ref[:, off:off + width])

        cqn = _rms(proj(_OFF_CQ, Q_LORA), gq_ref[...]).astype(BF16)
        q = _dot(cqn, wuq_ref[...])
        for h in range(N_HEADS):
            pair = q[:, (h // 2) * LANES:(h // 2 + 1) * LANES].astype(BF16)
            qlat_h = _dot(pair, wuk_ref[h]) * Q_SCALE
            qlat_ref[h, :, rows] = qlat_h.T.astype(BF16)
        qpe_ref[:, rows] = (_rope(q[:, N_HEADS * NOPE_DIM:], cos2, sin2) * Q_SCALE).T.astype(BF16)

        kv = _rms(proj(_OFF_CKV, KV_LORA), gkv_ref[...])
        kv_ref[rows, :] = kv
        kpe = _rope(proj(_OFF_KPE, PE_WIDE), cos2, sin2)
        pe_ref[rows, :] = kpe[:, :ROPE_DIM]
        kcat_ref[rows, :] = jnp.concatenate([kv.astype(BF16), kpe.astype(BF16)], axis=1)
        kv_t = kv.T.astype(BF16)
        for c in range(sub // KEY_CHUNK):
            kvt_ref[r * (sub // KEY_CHUNK) + c] = kv_t[:, c * KEY_CHUNK:(c + 1) * KEY_CHUNK]

        u = jax.nn.gelu(proj(_OFF_U, GV))
        v = jax.nn.gelu(proj(_OFF_V, GV))
        mu = jnp.mean(v, axis=-1, keepdims=True)
        vc = v - mu
        var = jnp.mean(vc * vc, axis=-1, keepdims=True)
        v_n = vc * lax.rsqrt(var + EPS) * lng_ref[...] + lnb_ref[...]
        if gv_ref is not None:
            gv_ref[rows, :] = v_n
        vnb = v_n.astype(BF16)
        tiles = []
        for c in range(sub // CHUNK):
            cols = []
            for jp in range(GV // LANES):
                vv = vnb[c * CHUNK:(c + 1) * CHUNK, jp * LANES:(jp + 1) * LANES]
                cols.append(jnp.where(low, _dot(wm[2 * jp], vv), _dot(wm[2 * jp + 1], vv)))
            tiles.append(jnp.concatenate(cols, axis=1) + bias)
        mixed = jnp.concatenate(tiles, axis=0)
        o_b = (u * mixed).astype(BF16)
        mb_ref[rows, :] = (jax.nn.sigmoid(proj(_OFF_GB, D_MODEL)) * _dot(o_b, wpb_ref[...])).astype(BF16)
        sa_ref[rows, :] = jax.nn.sigmoid(proj(_OFF_GA, D_MODEL)).astype(BF16)


def _mixin(x, mod, g_mix, w_in_r, g_q, w_uq_r, w_uk_p, g_kv, cos2, sin2, ln_g, ln_b, w_mix, b_mix, w_pb,
           *, rows_per_seq, chunk_len, want_gv, tm):
    G, T, _ = x.shape
    _check_tile(x, tm, rows_per_seq)
    grid = (G, T // tm)
    out_shape = [
        jax.ShapeDtypeStruct((G, T, KV_LORA), F32),
        jax.ShapeDtypeStruct((G, T, ROPE_DIM), F32),
        jax.ShapeDtypeStruct((G, T, KV_LORA + PE_WIDE), BF16),
        jax.ShapeDtypeStruct((G, T // KEY_CHUNK, KV_LORA, KEY_CHUNK), BF16),
        jax.ShapeDtypeStruct((G, N_HEADS, KV_LORA, T), BF16),
        jax.ShapeDtypeStruct((G, PE_WIDE, T), BF16),
        jax.ShapeDtypeStruct((G, T, D_MODEL), BF16),
        jax.ShapeDtypeStruct((G, T, D_MODEL), BF16),
    ]
    out_specs = [
        _row_spec(tm, KV_LORA),
        _row_spec(tm, ROPE_DIM),
        _row_spec(tm, KV_LORA + PE_WIDE),
        pl.BlockSpec((None, tm // KEY_CHUNK, KV_LORA, KEY_CHUNK), lambda g, i: (g, i, 0, 0)),
        pl.BlockSpec((None, N_HEADS, KV_LORA, tm), lambda g, i: (g, 0, 0, i)),
        pl.BlockSpec((None, PE_WIDE, tm), lambda g, i: (g, 0, i)),
        _row_spec(tm, D_MODEL),
        _row_spec(tm, D_MODEL),
    ]
    if want_gv:
        out_shape.append(jax.ShapeDtypeStruct((G, T, GV), F32))
        out_specs.append(_row_spec(tm, GV))
        body = functools.partial(_mixin_kernel, rows_per_seq=rows_per_seq, chunk_len=chunk_len)
    else:
        def body(*refs):
            _mixin_kernel(*refs, None, rows_per_seq=rows_per_seq, chunk_len=chunk_len)
    return pl.pallas_call(
        body,
        out_shape=out_shape,
        grid=grid,
        in_specs=[
            _row_spec(tm, D_MODEL),
            _mod_spec(mod, rows_per_seq),
            _const_spec((1, D_MODEL)),
            _const_spec(w_in_r.shape),
            _const_spec((1, Q_LORA)),
            _const_spec(w_uq_r.shape),
            _const_spec(w_uk_p.shape),
            _const_spec((1, KV_LORA)),
            pl.BlockSpec((tm, PE_WIDE), lambda g, i: (i, 0)),
            pl.BlockSpec((tm, PE_WIDE), lambda g, i: (i, 0)),
            _const_spec((1, GV)),
            _const_spec((1, GV)),
            _const_spec(w_mix.shape),
            _const_spec(b_mix.shape),
            _const_spec(w_pb.shape),
        ],
        out_specs=out_specs,
        compiler_params=_params(2),
        name="mixin",
    )(x, mod, g_mix, w_in_r, g_q, w_uq_r, w_uk_p, g_kv, cos2, sin2, ln_g, ln_b, w_mix, b_mix, w_pb)


_TQ = 256
_HEADS_PER_BLOCK = 1
_COL_BLOCK = _HEADS_PER_BLOCK * _TQ


def _pattn_kernel(qlat_ref, qpe_ref, kcat_ref, kvt_ref, o_ref, qt_sc, s0_sc, s1_sc, m_sc, l_sc, acc_sc):
    i = pl.program_id(1)
    tk = KEY_CHUNK
    head_of_row = lax.broadcasted_iota(jnp.int32, (PE_WIDE, _TQ), 0) // ROPE_DIM
    qp = qpe_ref[...]
    for h in range(N_HEADS):
        qt_sc[0:KV_LORA, h * _TQ:(h + 1) * _TQ] = qlat_ref[h]
        qt_sc[KV_LORA:, h * _TQ:(h + 1) * _TQ] = jnp.where(head_of_row == h, qp, jnp.zeros_like(qp))
    m_sc[...] = jnp.full(m_sc.shape, -1e30, F32)
    l_sc[...] = jnp.zeros(l_sc.shape, F32)
    acc_sc[...] = jnp.zeros(acc_sc.shape, F32)

    def scores(j, s_ref):
        k = kcat_ref[pl.ds(pl.multiple_of(j * tk, tk), tk), :]
        s_ref[...] = _dot(k, qt_sc[...])

    def accumulate(j, s_ref, masked):
        v_t = kvt_ref[j]
        for cb in range(N_HEADS // _HEADS_PER_BLOCK):
            cols = slice(cb * _COL_BLOCK, (cb + 1) * _COL_BLOCK)
            s = s_ref[:, cols]
            if masked:
                q_pos = i * _TQ + lax.broadcasted_iota(jnp.int32, s.shape, 1) % _TQ
                k_pos = j * tk + lax.broadcasted_iota(jnp.int32, s.shape, 0)
                s = jnp.where(k_pos <= q_pos, s, -jnp.inf)
            m_prev = m_sc[:, cols]
            m_new = jnp.maximum(m_prev, jnp.max(s, axis=0, keepdims=True))
            alpha = jnp.exp2(m_prev - m_new)
            p = jnp.exp2(s - m_new)
            l_sc[:, cols] = alpha * l_sc[:, cols] + jnp.sum(p, axis=0, keepdims=True)
            acc_sc[:, cols] = alpha * acc_sc[:, cols] + _dot(v_t, p.astype(BF16))
            m_sc[:, cols] = m_new

    n_full = (i * _TQ) // tk
    scores(0, s0_sc)

    def pair(p, carry):
        j = 2 * p
        scores(j + 1, s1_sc)
        accumulate(j, s0_sc, False)
        scores(j + 2, s0_sc)
        accumulate(j + 1, s1_sc, False)
        return carry

    lax.fori_loop(0, n_full // 2, pair, 0)

    @pl.when(n_full % 2 == 0)
    def _():
        accumulate(n_full, s0_sc, True)

    @pl.when(n_full % 2 == 1)
    def _():
        scores(n_full, s1_sc)
        accumulate(n_full - 1, s0_sc, False)
        accumulate(n_full, s1_sc, True)

    o_t = acc_sc[...] / l_sc[...]
    for h in range(N_HEADS):
        o_ref[h] = o_t[:, h * _TQ:(h + 1) * _TQ].T.astype(BF16)


def _pattn(q_lat_t, q_pe_t, kcat, kv_t):
    B, H, C, S = q_lat_t.shape
    cols = H * _TQ
    return pl.pallas_call(
        _pattn_kernel,
        out_shape=jax.ShapeDtypeStruct((B, H, S, C), BF16),
        grid=(B, S // _TQ),
        in_specs=[
            pl.BlockSpec((None, H, C, _TQ), lambda b, i: (b, 0, 0, i)),
            pl.BlockSpec((None, PE_WIDE, _TQ), lambda b, i: (b, 0, i)),
            pl.BlockSpec((None, S, C + PE_WIDE), lambda b, i: (b, 0, 0)),
            pl.BlockSpec((None, S // KEY_CHUNK, C, KEY_CHUNK), lambda b, i: (b, 0, 0, 0)),
        ],
        out_specs=pl.BlockSpec((None, H, _TQ, C), lambda b, i: (b, 0, i, 0)),
        scratch_shapes=[
            pltpu.VMEM((C + PE_WIDE, cols), BF16),
            pltpu.VMEM((KEY_CHUNK, cols), F32),
            pltpu.VMEM((KEY_CHUNK, cols), F32),
            pltpu.VMEM((1, cols), F32),
            pltpu.VMEM((1, cols), F32),
            pltpu.VMEM((C, cols), F32),
        ],
        compiler_params=_params(2),
        name="pattn",
    )(q_lat_t, q_pe_t, kcat, kv_t)


_PAGES_PER_CHUNK = 32


_DMA_SLOTS = 3


def _sattn_kernel(pt_ref, ql_ref, qp_ref, kvn_ref, pen_ref, ckv_hbm, cpet_hbm, o_ref,
                  kvbuf, pebuf, kvb16_0, kvb16_1, s0_sc, s1_sc, m_sc, l_sc, acc_sc, sem, *, n_pages, t_new):
    pc = _PAGES_PER_CHUNK
    n_chunks = n_pages // pc
    n_seq, rows, _ = ql_ref.shape
    total = n_seq * n_chunks
    assert n_chunks % 2 == 0 and total > _DMA_SLOTS
    work = ((kvb16_0, s0_sc), (kvb16_1, s1_sc))

    def copies(g, slot):
        b = g // n_chunks
        c = g % n_chunks
        out = []
        for p in range(pc):
            page = pt_ref[b, c * pc + p]
            dst = pl.ds(p * PAGE_SIZE, PAGE_SIZE)
            out.append(pltpu.make_async_copy(ckv_hbm.at[page], kvbuf.at[slot, dst], sem.at[0, slot]))
            out.append(pltpu.make_async_copy(cpet_hbm.at[page], pebuf.at[slot, :, dst], sem.at[1, slot]))
        return out

    def start(g):
        for cp in copies(g, g % _DMA_SLOTS):
            cp.start()

    def wait(g):
        for cp in copies(g, g % _DMA_SLOTS):
            cp.wait()

    def scores(g, w):
        kvb16, s_sc = work[w]
        b = g // n_chunks
        slot = g % _DMA_SLOTS
        wait(g)
        kvb = kvbuf[slot].astype(BF16)
        kvb16[...] = kvb
        peb = pebuf[slot].astype(BF16)
        s_sc[...] = _dot_nt(ql_ref[b], kvb) + _dot(qp_ref[b], peb)

    def reset_state():
        m_sc[...] = jnp.full(m_sc.shape, -1e30, F32)
        l_sc[...] = jnp.zeros(l_sc.shape, F32)
        acc_sc[...] = jnp.zeros(acc_sc.shape, F32)

    def accumulate(w):
        kvb16, s_sc = work[w]
        s = s_sc[...]
        m_prev = m_sc[...]
        m_new = jnp.maximum(m_prev, jnp.max(s, axis=-1, keepdims=True))
        alpha = jnp.exp2(m_prev - m_new)
        p = jnp.exp2(s - m_new)
        l_sc[...] = alpha * l_sc[...] + jnp.sum(p, axis=-1, keepdims=True)
        acc_sc[...] = alpha * acc_sc[...] + _dot(p.astype(BF16), kvb16[...])
        m_sc[...] = m_new

    def finish(b):
        qlf = ql_ref[b].astype(F32)
        qpf = qp_ref[b].astype(F32)
        kvn = kvn_ref[b]
        pen = pen_ref[b]
        t_of_row = lax.broadcasted_iota(jnp.int32, (rows, 1), 0) % t_new
        s_new = []
        for t in range(t_new):
            st = (jnp.sum(qlf * kvn[t:t + 1, :], axis=-1, keepdims=True)
                  + jnp.sum(qpf * pen[t:t + 1, :], axis=-1, keepdims=True))
            s_new.append(jnp.where(t <= t_of_row, st, -jnp.inf))
        m_prev = m_sc[...]
        m_new = m_prev
        for st in s_new:
            m_new = jnp.maximum(m_new, st)
        alpha = jnp.exp2(m_prev - m_new)
        l_new = alpha * l_sc[...]
        acc = alpha * acc_sc[...]
        for t, st in enumerate(s_new):
            pt = jnp.exp2(st - m_new)
            l_new = l_new + pt
            acc = acc + pt * kvn[t:t + 1, :]
        o_ref[b] = (acc / l_new).astype(o_ref.dtype)
        reset_state()

    reset_state()
    for g0 in range(_DMA_SLOTS):
        start(g0)
    scores(0, 0)

    def body(k, carry):
        g = 2 * k + 1
        start(g + _DMA_SLOTS - 1)
        scores(g, 1)
        accumulate(0)
        start(g + _DMA_SLOTS)
        scores(g + 1, 0)
        accumulate(1)

        @pl.when((g + 1) % n_chunks == 0)
        def _():
            finish((g + 1) // n_chunks - 1)

        return carry

    lax.fori_loop(0, total // 2 - 1 + pl.program_id(0), body, 0)
    scores(total - 1, 1)
    accumulate(0)
    accumulate(1)
    finish(n_seq - 1)
    wait(total)


def _sattn(page_table, q_lat, q_pe, kv_new, pe_new, cache_kv, cache_pe_t):
    DB, rows, C = q_lat.shape
    t_new = kv_new.shape[1]
    n_pages = page_table.shape[1]
    keys = _PAGES_PER_CHUNK * PAGE_SIZE
    page_table = jnp.concatenate([page_table, page_table[:1]], axis=0)
    grid_spec = pltpu.PrefetchScalarGridSpec(
        num_scalar_prefetch=1,
        grid=(1,),
        in_specs=[
            _const_spec(q_lat.shape),
            _const_spec(q_pe.shape),
            _const_spec(kv_new.shape),
            _const_spec(pe_new.shape),
            pl.BlockSpec(memory_space=pl.ANY),
            pl.BlockSpec(memory_space=pl.ANY),
        ],
        out_specs=pl.BlockSpec((DB, rows, C), lambda i, pt: (0, 0, 0)),
        scratch_shapes=[
            pltpu.VMEM((_DMA_SLOTS, keys, C), F32),
            pltpu.VMEM((_DMA_SLOTS, ROPE_DIM, keys), F32),
            pltpu.VMEM((keys, C), BF16),
            pltpu.VMEM((keys, C), BF16),
            pltpu.VMEM((rows, keys), F32),
            pltpu.VMEM((rows, keys), F32),
            pltpu.VMEM((rows, 1), F32),
            pltpu.VMEM((rows, 1), F32),
            pltpu.VMEM((rows, C), F32),
            pltpu.SemaphoreType.DMA((2, _DMA_SLOTS)),
        ],
    )
    return pl.pallas_call(
        functools.partial(_sattn_kernel, n_pages=n_pages, t_new=t_new),
        out_shape=jax.ShapeDtypeStruct((DB, rows, C), BF16),
        grid_spec=grid_spec,
        compiler_params=_params(1),
        name="sattn",
    )(page_table, q_lat, q_pe, kv_new, pe_new, cache_kv, cache_pe_t)


def _rope_tables(pos):
    half = ROPE_DIM // 2
    freqs = ROPE_THETA ** (-2.0 * jnp.arange(half, dtype=F32) / ROPE_DIM)
    ang = pos.astype(F32)[:, None] * freqs[None, :]
    cos = jnp.cos(ang)
    sin = jnp.sin(ang)
    cos2 = jnp.tile(jnp.concatenate([cos, cos], axis=1), (1, N_HEADS))
    sin2 = jnp.tile(jnp.concatenate([-sin, sin], axis=1), (1, N_HEADS))
    return cos2, sin2


def _pair_pad(w, axis):
    z = jnp.zeros_like(w)
    even = jnp.concatenate([w, z], axis=axis)
    odd = jnp.concatenate([z, w], axis=axis)
    sel = (jnp.arange(w.shape[0]) % 2 == 0).reshape(-1, 1, 1)
    return jnp.where(sel, even, odd)


def kernel(x_prompt, x_sample, cache_kv, cache_pe, page_table, c_prompt, c_sample, w_ada, b_ada, g_ffn1, w1_gate, w1_up, w1_down, g_mix, w_in, g_q, w_uq, g_kv, w_uk, w_uv, ln_v_g, ln_v_b, w_s, b_s, w_pa, w_pb, w_o, g_ffn2, w2_gate, w2_up, w2_down, g_final):
    B, S, D = x_prompt.shape
    DB, T, _ = x_sample.shape
    depth = w_ada.shape[0]
    assert depth == 1
    l = 0
    R = DB * T

    row = lambda v: v.reshape(1, -1)
    wi = w_in[l]
    seg = lambda a, n: wi[:, a:a + n]
    o_ckv = Q_LORA
    o_kpe = o_ckv + KV_LORA
    o_u = o_kpe + ROPE_DIM
    o_v = o_u + GV
    o_ga = o_v + GV
    o_gb = o_ga + D_MODEL
    w_in_r = jnp.concatenate([
        seg(0, Q_LORA), seg(o_ckv, KV_LORA), jnp.tile(seg(o_kpe, ROPE_DIM), (1, N_HEADS)),
        seg(o_u, GV), seg(o_v, GV), seg(o_ga, D_MODEL), seg(o_gb, D_MODEL)], axis=1).astype(BF16)
    wq = w_uq[l].reshape(Q_LORA, N_HEADS, NOPE_DIM + ROPE_DIM)
    w_uq_r = jnp.concatenate([wq[:, :, :NOPE_DIM].reshape(Q_LORA, -1),
                              wq[:, :, NOPE_DIM:].reshape(Q_LORA, -1)], axis=1).astype(BF16)
    w_uk_p = _pair_pad(jnp.transpose(w_uk[l], (1, 2, 0)), 1).astype(BF16)
    w_uv_p = _pair_pad(jnp.transpose(w_uv[l], (1, 0, 2)), 2).astype(BF16)
    bf = lambda w: w[l].astype(BF16)
    w1g, w1u, w1d = bf(w1_gate), bf(w1_up), bf(w1_down)
    w2g, w2u, w2d = bf(w2_gate), bf(w2_up), bf(w2_down)
    wpa, wpb, wo = bf(w_pa), bf(w_pb), bf(w_o)
    gf = row(g_final)

    w_mix_p = w_s[l]
    b_mix_p = jnp.repeat(b_s[l].T, GROUP_DIM, axis=1)
    reps = CHUNK // T
    w_mix_s = jnp.tile(w_s[l][:, :T, :T], (1, reps, reps))
    b_mix_s = jnp.repeat(jnp.tile(b_s[l][:, :T].T, (reps, 1)), GROUP_DIM, axis=1)

    mod = _ada(jnp.concatenate([c_prompt, c_sample], axis=0), w_ada[l], b_ada[l])
    mod_p = mod[:B].reshape(B, N_SUB, 3, D)
    mod_s = jnp.transpose(mod[B:].reshape(DB, N_SUB, 3, D), (1, 2, 0, 3))

    past_len = page_table.shape[1] * PAGE_SIZE
    cos_p, sin_p = _rope_tables(jnp.arange(S))
    cos_s, sin_s = _rope_tables(past_len + jnp.arange(R) % T)

    def mixin(x1, mod, rows_per_seq, tm, chunk_len, cos2, sin2, w_mix, b_mix, want_gv):
        return _mixin(x1, mod, row(g_mix[l]), w_in_r, row(g_q[l]), w_uq_r, w_uk_p, row(g_kv[l]),
                      cos2, sin2, row(ln_v_g[l]), row(ln_v_b[l]), w_mix, b_mix, wpb,
                      rows_per_seq=rows_per_seq, chunk_len=chunk_len, want_gv=want_gv, tm=tm)

    def merge_ffn2(x1, mod_mix, mod_ffn, o_lat, sa, mb, rows_per_seq, tm):
        return _ffn(x1, mod_ffn, row(g_ffn2[l]), w2g, w2u, w2d, gf,
                    rows_per_seq=rows_per_seq, final_norm=True, tm=tm,
                    merge_args=(o_lat, sa, mb, mod_mix, w_uv_p, wpa, wo))

    tm_p = 512
    x1_s = _ffn(x_sample.reshape(1, R, D), mod_s[0], row(g_ffn1[l]), w1g, w1u, w1d, gf,
                rows_per_seq=T, final_norm=False, tm=R)
    kv_s, pe_s, _, _, q_lat_t, q_pe_t, sa_s, mb_s, gv_s = mixin(x1_s, mod_s[1], T, R, T, cos_s, sin_s,
                                                                w_mix_s, b_mix_s, True)
    ql = jnp.transpose(q_lat_t.reshape(N_HEADS, KV_LORA, DB, T), (2, 0, 3, 1)).reshape(DB, N_HEADS * T, KV_LORA)
    qp = jnp.transpose(q_pe_t.reshape(N_HEADS, ROPE_DIM, DB, T), (2, 0, 3, 1)).reshape(DB, N_HEADS * T, ROPE_DIM)

    cache_pe_t = jnp.swapaxes(cache_pe[l], 1, 2)
    o_s = _sattn(page_table, ql, qp, kv_s.reshape(DB, T, KV_LORA), pe_s.reshape(DB, T, ROPE_DIM),
                 cache_kv[l], cache_pe_t)

    x1_p = _ffn(x_prompt, mod_p[:, 0], row(g_ffn1[l]), w1g, w1u, w1d, gf,
                rows_per_seq=0, final_norm=False, tm=tm_p)
    kv_p, pe_p, kcat, kv_t, q_lat_t, q_pe_t, sa_p, mb_p = mixin(x1_p, mod_p[:, 1], 0, tm_p, CHUNK, cos_p, sin_p,
                                                                w_mix_p, b_mix_p, False)
    o_lat_p = _pattn(q_lat_t, q_pe_t, kcat, kv_t)
    y_p = merge_ffn2(x1_p, mod_p[:, 1], mod_p[:, 2], o_lat_p, sa_p, mb_p, 0, tm_p)

    o_lat_s = jnp.transpose(o_s.reshape(DB, N_HEADS, T, KV_LORA), (1, 0, 2, 3)).reshape(1, N_HEADS, R, KV_LORA)
    y_s = merge_ffn2(x1_s, mod_s[1], mod_s[2], o_lat_s, sa_s, mb_s, T, R)

    return (y_p, y_s.reshape(DB, T, D),
            kv_p.reshape(1, B, S, KV_LORA), pe_p.reshape(1, B, S, ROPE_DIM),
            kv_s.reshape(1, DB, T, KV_LORA), pe_s.reshape(1, DB, T, ROPE_DIM),
            gv_s.reshape(1, DB, T, GV))
```

```python
import functools

import jax
import jax.numpy as jnp
from jax import lax
from jax.experimental import pallas as pl
from jax.experimental.pallas import tpu as pltpu

D_MODEL = 1024
N_HEADS = 8
Q_LORA = 256
KV_LORA = 256
NOPE_DIM = 64
ROPE_DIM = 32
V_DIM = 64
ROPE_THETA = 10000.0
SM_SCALE = (NOPE_DIM + ROPE_DIM) ** -0.5
Q_SCALE = SM_SCALE * 1.4426950408889634
CHUNK = 128
GV = 512
N_GROUPS = 8
GROUP_DIM = GV // N_GROUPS
D_FF = 2816
N_SUB = 3
EPS = 1e-6
PAGE_SIZE = 128
PE_WIDE = N_HEADS * ROPE_DIM
KEY_CHUNK = 256

VMEM_LIMIT_BYTES = 56 * 1024 * 1024
LANES = 128

BF16 = jnp.bfloat16
F32 = jnp.float32

_OFF_CQ = 0
_OFF_CKV = _OFF_CQ + Q_LORA
_OFF_KPE = _OFF_CKV + KV_LORA
_OFF_U = _OFF_KPE + PE_WIDE
_OFF_V = _OFF_U + GV
_OFF_GA = _OFF_V + GV
_OFF_GB = _OFF_GA + D_MODEL
_N_IN_R = _OFF_GB + D_MODEL


def _dot(a, b):
    return jnp.dot(a, b, preferred_element_type=F32)


def _dot_nt(a, b):
    return lax.dot_general(a, b, (((1,), (1,)), ((), ())), preferred_element_type=F32)


def _rms(x, g):
    return x * lax.rsqrt(jnp.mean(x * x, axis=-1, keepdims=True) + EPS) * g


def _row_expander(tm, n_seq, rows_per_seq, row0=0):
    if not rows_per_seq:
        return None
    r = lax.broadcasted_iota(jnp.int32, (tm, n_seq), 0) + row0
    s = lax.broadcasted_iota(jnp.int32, (tm, n_seq), 1)
    return (r // rows_per_seq == s).astype(F32)


def _mod_row(mod_ref, k, expand):
    if expand is None:
        return mod_ref[k:k + 1, :]
    return jnp.dot(expand, mod_ref[k], precision=lax.Precision.HIGHEST, preferred_element_type=F32)


def _const_spec(shape):
    nd = len(shape)
    return pl.BlockSpec(shape, lambda *_: (0,) * nd, pipeline_mode=pl.Buffered(1))


def _params(n_grid, flags=None):
    return pltpu.CompilerParams(
        dimension_semantics=("arbitrary",) * n_grid,
        vmem_limit_bytes=VMEM_LIMIT_BYTES,
        flags=flags,
    )


def _ada_kernel(c_ref, w_ref, b_ref, o_ref):
    c = c_ref[...]
    a = (c * jax.nn.sigmoid(c)).astype(BF16)
    o_ref[...] = _dot(a, w_ref[...].astype(BF16)) + b_ref[...]


def _ada(c_all, w_ada, b_ada):
    rows, d = c_all.shape
    n = w_ada.shape[1]
    tn = 1024
    return pl.pallas_call(
        _ada_kernel,
        out_shape=jax.ShapeDtypeStruct((rows, n), F32),
        grid=(n // tn,),
        in_specs=[
            pl.BlockSpec((rows, d), lambda j: (0, 0)),
            pl.BlockSpec((d, tn), lambda j: (0, j)),
            pl.BlockSpec((1, tn), lambda j: (0, j)),
        ],
        out_specs=pl.BlockSpec((rows, tn), lambda j: (0, j)),
        compiler_params=_params(1),
        name="ada",
    )(c_all, w_ada, b_ada.reshape(1, n))


_FF_CHUNKS = ((0, 1536), (1536, 1280))


def _merge_mixers(x, ol_ref, sa_ref, mb_ref, gate, wuv_ref, wpa_ref, wo_ref):
    parts = []
    for jp in range(N_HEADS // 2):
        a = _dot(ol_ref[2 * jp], wuv_ref[2 * jp]) + _dot(ol_ref[2 * jp + 1], wuv_ref[2 * jp + 1])
        parts.append(a.astype(BF16))
    o_a = jnp.concatenate(parts, axis=1)
    merged = sa_ref[...].astype(F32) * _dot(o_a, wpa_ref[...]) + mb_ref[...].astype(F32)
    return x + gate * _dot(merged.astype(BF16), wo_ref[...])


def _ffn_stages(x_fn, mod_ref, g_ref, wg_ref, wu_ref, wd_ref, gf_ref, o_ref, expand, final_norm):
    st = {}

    def norm():
        x = x_fn()
        shift = _mod_row(mod_ref, 0, expand)
        scale = _mod_row(mod_ref, 1, expand)
        st["x"] = x
        st["gate"] = _mod_row(mod_ref, 2, expand)
        st["nb"] = (_rms(x, g_ref[...]) * (1.0 + scale) + shift).astype(BF16)

    def gate_proj(c):
        off, width = _FF_CHUNKS[c]
        st["hg"] = _dot(st["nb"], wg_ref[:, off:off + width])

    def up_proj(c):
        off, width = _FF_CHUNKS[c]
        hg = st.pop("hg")
        st["h"] = (hg * jax.nn.sigmoid(hg) * _dot(st["nb"], wu_ref[:, off:off + width])).astype(BF16)

    def down_proj(c):
        off, width = _FF_CHUNKS[c]
        d = _dot(st.pop("h"), wd_ref[off:off + width, :])
        st["acc"] = d if c == 0 else st["acc"] + d

    def final():
        y = st["x"] + 0.5 * st["gate"] * st["acc"]
        if final_norm:
            y = _rms(y, gf_ref[...])
        o_ref[...] = y

    stages = [norm]
    for c in range(len(_FF_CHUNKS)):
        stages += [functools.partial(gate_proj, c), functools.partial(up_proj, c), functools.partial(down_proj, c)]
    return stages + [final]


def _ffn_kernel(*refs, rows_per_seq, final_norm, merge):
    if merge:
        (x_ref, ol_ref, sa_ref, mb_ref, modm_ref, wuv_ref, wpa_ref, wo_ref,
         mod_ref, g_ref, wg_ref, wu_ref, wd_ref, gf_ref, o_ref) = refs
    else:
        x_ref, mod_ref, g_ref, wg_ref, wu_ref, wd_ref, gf_ref, o_ref = refs
    expand = _row_expander(x_ref.shape[0], mod_ref.shape[1], rows_per_seq)

    def x_fn():
        x = x_ref[...]
        if merge:
            x = _merge_mixers(x, ol_ref, sa_ref, mb_ref, _mod_row(modm_ref, 2, expand), wuv_ref, wpa_ref, wo_ref)
        return x

    for stage in _ffn_stages(x_fn, mod_ref, g_ref, wg_ref, wu_ref, wd_ref, gf_ref, o_ref, expand, final_norm):
        stage()


def _mod_spec(mod, rows_per_seq):
    if rows_per_seq:
        return pl.BlockSpec(mod.shape, lambda g, i, *_: (0, 0, 0))
    return pl.BlockSpec((None, 3, D_MODEL), lambda g, i, *_: (g, 0, 0))


def _row_spec(tm, width):
    return pl.BlockSpec((None, tm, width), lambda g, i, *_: (g, i, 0))


def _check_tile(x, tm, rows_per_seq):
    G, T, _ = x.shape
    assert T % tm == 0
    if rows_per_seq:
        assert G == 1 and tm == T, "per-sequence modulation is expanded for one tile holding every row"


def _ffn(x, mod, g, wg, wu, wd, g_final, *, rows_per_seq, final_norm, tm, merge_args=None):
    G, T, _ = x.shape
    _check_tile(x, tm, rows_per_seq)
    operands = [x]
    in_specs = [_row_spec(tm, D_MODEL)]
    if merge_args is not None:
        o_lat, sa, mb, mod_mix, w_uv_p, w_pa, w_o = merge_args
        operands += [o_lat, sa, mb, mod_mix, w_uv_p, w_pa, w_o]
        in_specs += [
            pl.BlockSpec((None, N_HEADS, tm, KV_LORA), lambda g, i: (g, 0, i, 0)),
            _row_spec(tm, D_MODEL),
            _row_spec(tm, D_MODEL),
            _mod_spec(mod_mix, rows_per_seq),
            _const_spec(w_uv_p.shape),
            _const_spec(w_pa.shape),
            _const_spec(w_o.shape),
        ]
    operands += [mod, g, wg, wu, wd, g_final]
    in_specs += [
        _mod_spec(mod, rows_per_seq),
        _const_spec((1, D_MODEL)),
        _const_spec(wg.shape),
        _const_spec(wu.shape),
        _const_spec(wd.shape),
        _const_spec((1, D_MODEL)),
    ]
    return pl.pallas_call(
        functools.partial(_ffn_kernel, rows_per_seq=rows_per_seq, final_norm=final_norm,
                          merge=merge_args is not None),
        out_shape=jax.ShapeDtypeStruct(x.shape, F32),
        grid=(G, T // tm),
        in_specs=in_specs,
        out_specs=_row_spec(tm, D_MODEL),
        compiler_params=_params(2),
        name="merge_ffn" if merge_args is not None else "ffn",
    )(*operands)


_MIXIN_SUB_ROWS = 512


def _rope(x, cos2, sin2):
    halves = []
    for k in range(x.shape[1] // LANES):
        xs = x[:, k * LANES:(k + 1) * LANES]
        lane = lax.broadcasted_iota(jnp.int32, xs.shape, 1)
        first = (lane % ROPE_DIM) < (ROPE_DIM // 2)
        swapped = jnp.where(first, pltpu.roll(xs, LANES - ROPE_DIM // 2, 1), pltpu.roll(xs, ROPE_DIM // 2, 1))
        halves.append(swapped)
    swapped = jnp.concatenate(halves, axis=1)
    return x * cos2 + swapped * sin2


def _mixin_kernel(x_ref, mod_ref, g_ref, win_ref, gq_ref, wuq_ref, wuk_ref, gkv_ref, cos_ref, sin_ref,
                  lng_ref, lnb_ref, wmix_ref, bmix_ref, wpb_ref,
                  kv_ref, pe_ref, kcat_ref, kvt_ref, qlat_ref, qpe_ref, sa_ref, mb_ref, gv_ref,
                  *, rows_per_seq, chunk_len):
    tm = x_ref.shape[0]
    row = lax.broadcasted_iota(jnp.int32, (CHUNK, CHUNK), 0)
    col = lax.broadcasted_iota(jnp.int32, (CHUNK, CHUNK), 1)
    keep = (col <= row) & ((row // chunk_len) == (col // chunk_len))
    wm = [jnp.where(keep, wmix_ref[g], 0.0).astype(BF16) for g in range(N_GROUPS)]
    lane = lax.broadcasted_iota(jnp.int32, (CHUNK, LANES), 1)
    low = lane < GROUP_DIM
    bias = bmix_ref[...]

    sub = _MIXIN_SUB_ROWS
    for r in range(tm // sub):
        rows = slice(r * sub, (r + 1) * sub)
        x = x_ref[rows, :]
        expand = _row_expander(sub, mod_ref.shape[1], rows_per_seq, r * sub)
        shift = _mod_row(mod_ref, 0, expand)
        scale = _mod_row(mod_ref, 1, expand)
        nb = (_rms(x, g_ref[...]) * (1.0 + scale) + shift).astype(BF16)
        cos2 = cos_ref[rows, :]
        sin2 = sin_ref[rows, :]

        def proj(off, width, nb=nb):
            return _dot(nb, win_ref[:, off:off + width])

        cqn = _rms(proj(_OFF_CQ, Q_LORA), gq_ref[...]).astype(BF16)
        q = _dot(cqn, wuq_ref[...])
        for h in range(N_HEADS):
            pair = q[:, (h // 2) * LANES:(h // 2 + 1) * LANES].astype(BF16)
            qlat_h = _dot(pair, wuk_ref[h]) * Q_SCALE
            qlat_ref[h, :, rows] = qlat_h.T.astype(BF16)
        qpe_ref[:, rows] = (_rope(q[:, N_HEADS * NOPE_DIM:], cos2, sin2) * Q_SCALE).T.astype(BF16)

        kv = _rms(proj(_OFF_CKV, KV_LORA), gkv_ref[...])
        kv_ref[rows, :] = kv
        kpe = _rope(proj(_OFF_KPE, PE_WIDE), cos2, sin2)
        pe_ref[rows, :] = kpe[:, :ROPE_DIM]
        kcat_ref[rows, :] = jnp.concatenate([kv.astype(BF16), kpe.astype(BF16)], axis=1)
        kv_t = kv.T.astype(BF16)
        for c in range(sub // KEY_CHUNK):
            kvt_ref[r * (sub // KEY_CHUNK) + c] = kv_t[:, c * KEY_CHUNK:(c + 1) * KEY_CHUNK]

        u = jax.nn.gelu(proj(_OFF_U, GV))
        v = jax.nn.gelu(proj(_OFF_V, GV))
        mu = jnp.mean(v, axis=-1, keepdims=True)
        vc = v - mu
        var = jnp.mean(vc * vc, axis=-1, keepdims=True)
        v_n = vc * lax.rsqrt(var + EPS) * lng_ref[...] + lnb_ref[...]
        if gv_ref is not None:
            gv_ref[rows, :] = v_n
        vnb = v_n.astype(BF16)
        tiles = []
        for c in range(sub // CHUNK):
            cols = []
            for jp in range(GV // LANES):
                vv = vnb[c * CHUNK:(c + 1) * CHUNK, jp * LANES:(jp + 1) * LANES]
                cols.append(jnp.where(low, _dot(wm[2 * jp], vv), _dot(wm[2 * jp + 1], vv)))
            tiles.append(jnp.concatenate(cols, axis=1) + bias)
        mixed = jnp.concatenate(tiles, axis=0)
        o_b = (u * mixed).astype(BF16)
        mb_ref[rows, :] = (jax.nn.sigmoid(proj(_OFF_GB, D_MODEL)) * _dot(o_b, wpb_ref[...])).astype(BF16)
        sa_ref[rows, :] = jax.nn.sigmoid(proj(_OFF_GA, D_MODEL)).astype(BF16)


def _mixin(x, mod, g_mix, w_in_r, g_q, w_uq_r, w_uk_p, g_kv, cos2, sin2, ln_g, ln_b, w_mix, b_mix, w_pb,
           *, rows_per_seq, chunk_len, want_gv, tm):
    G, T, _ = x.shape
    _check_tile(x, tm, rows_per_seq)
    grid = (G, T // tm)
    out_shape = [
        jax.ShapeDtypeStruct((G, T, KV_LORA), F32),
        jax.ShapeDtypeStruct((G, T, ROPE_DIM), F32),
        jax.ShapeDtypeStruct((G, T, KV_LORA + PE_WIDE), BF16),
        jax.ShapeDtypeStruct((G, T // KEY_CHUNK, KV_LORA, KEY_CHUNK), BF16),
        jax.ShapeDtypeStruct((G, N_HEADS, KV_LORA, T), BF16),
        jax.ShapeDtypeStruct((G, PE_WIDE, T), BF16),
        jax.ShapeDtypeStruct((G, T, D_MODEL), BF16),
        jax.ShapeDtypeStruct((G, T, D_MODEL), BF16),
    ]
    out_specs = [
        _row_spec(tm, KV_LORA),
        _row_spec(tm, ROPE_DIM),
        _row_spec(tm, KV_LORA + PE_WIDE),
        pl.BlockSpec((None, tm // KEY_CHUNK, KV_LORA, KEY_CHUNK), lambda g, i: (g, i, 0, 0)),
        pl.BlockSpec((None, N_HEADS, KV_LORA, tm), lambda g, i: (g, 0, 0, i)),
        pl.BlockSpec((None, PE_WIDE, tm), lambda g, i: (g, 0, i)),
        _row_spec(tm, D_MODEL),
        _row_spec(tm, D_MODEL),
    ]
    if want_gv:
        out_shape.append(jax.ShapeDtypeStruct((G, T, GV), F32))
        out_specs.append(_row_spec(tm, GV))
        body = functools.partial(_mixin_kernel, rows_per_seq=rows_per_seq, chunk_len=chunk_len)
    else:
        def body(*refs):
            _mixin_kernel(*refs, None, rows_per_seq=rows_per_seq, chunk_len=chunk_len)
    return pl.pallas_call(
        body,
        out_shape=out_shape,
        grid=grid,
        in_specs=[
            _row_spec(tm, D_MODEL),
            _mod_spec(mod, rows_per_seq),
            _const_spec((1, D_MODEL)),
            _const_spec(w_in_r.shape),
            _const_spec((1, Q_LORA)),
            _const_spec(w_uq_r.shape),
            _const_spec(w_uk_p.shape),
            _const_spec((1, KV_LORA)),
            pl.BlockSpec((tm, PE_WIDE), lambda g, i: (i, 0)),
            pl.BlockSpec((tm, PE_WIDE), lambda g, i: (i, 0)),
            _const_spec((1, GV)),
            _const_spec((1, GV)),
            _const_spec(w_mix.shape),
            _const_spec(b_mix.shape),
            _const_spec(w_pb.shape),
        ],
        out_specs=out_specs,
        compiler_params=_params(2),
        name="mixin",
    )(x, mod, g_mix, w_in_r, g_q, w_uq_r, w_uk_p, g_kv, cos2, sin2, ln_g, ln_b, w_mix, b_mix, w_pb)


_TQ = 256
_COL_BLOCK = 256


def _pattn_kernel(qlat_ref, qpe_ref, kcat_ref, kvt_ref, o_ref, qt_sc, s0_sc, s1_sc, m_sc, l_sc, acc_sc):
    i = pl.program_id(1)
    tk = KEY_CHUNK
    head_of_row = lax.broadcasted_iota(jnp.int32, (PE_WIDE, _TQ), 0) // ROPE_DIM
    qp = qpe_ref[...]
    for h in range(N_HEADS):
        qt_sc[0:KV_LORA, h * _TQ:(h + 1) * _TQ] = qlat_ref[h]
        qt_sc[KV_LORA:, h * _TQ:(h + 1) * _TQ] = jnp.where(head_of_row == h, qp, jnp.zeros_like(qp))
    m_sc[...] = jnp.full(m_sc.shape, -1e30, F32)
    l_sc[...] = jnp.zeros(l_sc.shape, F32)
    acc_sc[...] = jnp.zeros(acc_sc.shape, F32)

    def scores(j, s_ref):
        k = kcat_ref[pl.ds(pl.multiple_of(j * tk, tk), tk), :]
        s_ref[...] = _dot(k, qt_sc[...])

    def accumulate(j, s_ref, masked):
        v_t = kvt_ref[j]
        for cb in range(N_HEADS * _TQ // _COL_BLOCK):
            cols = slice(cb * _COL_BLOCK, (cb + 1) * _COL_BLOCK)
            s = s_ref[:, cols]
            if masked:
                col = cb * _COL_BLOCK + lax.broadcasted_iota(jnp.int32, s.shape, 1)
                q_pos = i * _TQ + col % _TQ
                k_pos = j * tk + lax.broadcasted_iota(jnp.int32, s.shape, 0)
                s = jnp.where(k_pos <= q_pos, s, -jnp.inf)
            m_prev = m_sc[:, cols]
            m_new = jnp.maximum(m_prev, jnp.max(s, axis=0, keepdims=True))
            alpha = jnp.exp2(m_prev - m_new)
            p = jnp.exp2(s - m_new)
            l_sc[:, cols] = alpha * l_sc[:, cols] + jnp.sum(p, axis=0, keepdims=True)
            acc_sc[:, cols] = alpha * acc_sc[:, cols] + _dot(v_t, p.astype(BF16))
            m_sc[:, cols] = m_new

    n_full = (i * _TQ) // tk
    scores(0, s0_sc)

    def pair(p, carry):
        j = 2 * p
        scores(j + 1, s1_sc)
        accumulate(j, s0_sc, False)
        scores(j + 2, s0_sc)
        accumulate(j + 1, s1_sc, False)
        return carry

    lax.fori_loop(0, n_full // 2, pair, 0)

    assert _TQ <= tk, "only chunk n_full may hold masked keys"

    @pl.when(n_full % 2 == 0)
    def _():
        accumulate(n_full, s0_sc, True)

    @pl.when(n_full % 2 == 1)
    def _():
        scores(n_full, s1_sc)
        accumulate(n_full - 1, s0_sc, False)
        accumulate(n_full, s1_sc, True)

    o_t = acc_sc[...] / l_sc[...]
    for h in range(N_HEADS):
        o_ref[h] = o_t[:, h * _TQ:(h + 1) * _TQ].T.astype(BF16)


def _pattn(q_lat_t, q_pe_t, kcat, kv_t):
    B, H, C, S = q_lat_t.shape
    cols = H * _TQ
    return pl.pallas_call(
        _pattn_kernel,
        out_shape=jax.ShapeDtypeStruct((B, H, S, C), BF16),
        grid=(B, S // _TQ),
        in_specs=[
            pl.BlockSpec((None, H, C, _TQ), lambda b, i: (b, 0, 0, i)),
            pl.BlockSpec((None, PE_WIDE, _TQ), lambda b, i: (b, 0, i)),
            pl.BlockSpec((None, S, C + PE_WIDE), lambda b, i: (b, 0, 0)),
            pl.BlockSpec((None, S // KEY_CHUNK, C, KEY_CHUNK), lambda b, i: (b, 0, 0, 0)),
        ],
        out_specs=pl.BlockSpec((None, H, _TQ, C), lambda b, i: (b, 0, i, 0)),
        scratch_shapes=[
            pltpu.VMEM((C + PE_WIDE, cols), BF16),
            pltpu.VMEM((KEY_CHUNK, cols), F32),
            pltpu.VMEM((KEY_CHUNK, cols), F32),
            pltpu.VMEM((1, cols), F32),
            pltpu.VMEM((1, cols), F32),
            pltpu.VMEM((C, cols), F32),
        ],
        compiler_params=_params(2),
        name="pattn",
    )(q_lat_t, q_pe_t, kcat, kv_t)


_PAGES_PER_CHUNK = 32


_DMA_SLOTS = 3


def _sattn_kernel(pt_ref, ql_ref, qp_ref, kvn_ref, pen_ref, ckv_hbm, cpet_hbm, o_ref,
                  kvbuf, pebuf, kvb16_0, kvb16_1, s0_sc, s1_sc, kt_sc, m_sc, l_sc, acc_sc, sem, *, n_pages, t_new):
    pc = _PAGES_PER_CHUNK
    n_chunks = n_pages // pc
    n_seq, rows, _ = ql_ref.shape
    total = n_seq * n_chunks
    assert n_chunks % 2 == 0 and total > _DMA_SLOTS
    work = ((kvb16_0, s0_sc), (kvb16_1, s1_sc))

    def copies(g, slot):
        b = g // n_chunks
        c = g % n_chunks
        out = []
        for p in range(pc):
            page = pt_ref[b, c * pc + p]
            dst = pl.ds(p * PAGE_SIZE, PAGE_SIZE)
            out.append(pltpu.make_async_copy(ckv_hbm.at[page], kvbuf.at[slot, dst], sem.at[0, slot]))
            out.append(pltpu.make_async_copy(cpet_hbm.at[page], pebuf.at[slot, :, dst], sem.at[1, slot]))
        return out

    def start(g):
        for cp in copies(g, g % _DMA_SLOTS):
            cp.start()

    def wait(g):
        for cp in copies(g, g % _DMA_SLOTS):
            cp.wait()

    def scores(g, w):
        kvb16, s_sc = work[w]
        b = g // n_chunks
        slot = g % _DMA_SLOTS
        wait(g)
        kvb = kvbuf[slot].astype(BF16)
        kvb16[...] = kvb
        peb = pebuf[slot].astype(BF16)
        kt_sc[...] = kvb.T
        s_sc[...] = _dot(ql_ref[b], kt_sc[...]) + _dot(qp_ref[b], peb)

    def reset_state():
        m_sc[...] = jnp.full(m_sc.shape, -1e30, F32)
        l_sc[...] = jnp.zeros(l_sc.shape, F32)
        acc_sc[...] = jnp.zeros(acc_sc.shape, F32)

    def accumulate(w):
        kvb16, s_sc = work[w]
        s = s_sc[...]
        m_prev = m_sc[...]
        m_new = jnp.maximum(m_prev, jnp.max(s, axis=-1, keepdims=True))
        alpha = jnp.exp2(m_prev - m_new)
        p = jnp.exp2(s - m_new)
        l_sc[...] = alpha * l_sc[...] + jnp.sum(p, axis=-1, keepdims=True)
        acc_sc[...] = alpha * acc_sc[...] + _dot(p.astype(BF16), kvb16[...])
        m_sc[...] = m_new

    def finish(b):
        qlf = ql_ref[b].astype(F32)
        qpf = qp_ref[b].astype(F32)
        kvn = kvn_ref[b]
        pen = pen_ref[b]
        t_of_row = lax.broadcasted_iota(jnp.int32, (rows, 1), 0) % t_new
        s_new = []
        for t in range(t_new):
            st = (jnp.sum(qlf * kvn[t:t + 1, :], axis=-1, keepdims=True)
                  + jnp.sum(qpf * pen[t:t + 1, :], axis=-1, keepdims=True))
            s_new.append(jnp.where(t <= t_of_row, st, -jnp.inf))
        m_prev = m_sc[...]
        m_new = m_prev
        for st in s_new:
            m_new = jnp.maximum(m_new, st)
        alpha = jnp.exp2(m_prev - m_new)
        l_new = alpha * l_sc[...]
        acc = alpha * acc_sc[...]
        for t, st in enumerate(s_new):
            pt = jnp.exp2(st - m_new)
            l_new = l_new + pt
            acc = acc + pt * kvn[t:t + 1, :]
        o_ref[b] = (acc / l_new).astype(o_ref.dtype)
        reset_state()

    reset_state()
    for g0 in range(_DMA_SLOTS):
        start(g0)
    scores(0, 0)

    def body(k, carry):
        g = 2 * k + 1
        start(g + _DMA_SLOTS - 1)
        scores(g, 1)
        accumulate(0)
        start(g + _DMA_SLOTS)
        scores(g + 1, 0)
        accumulate(1)

        @pl.when((g + 1) % n_chunks == 0)
        def _():
            finish((g + 1) // n_chunks - 1)

        return carry

    lax.fori_loop(0, total // 2 - 1 + pl.program_id(0), body, 0)
    scores(total - 1, 1)
    accumulate(0)
    accumulate(1)
    finish(n_seq - 1)
    wait(total)


def _sattn(page_table, q_lat, q_pe, kv_new, pe_new, cache_kv, cache_pe_t):
    DB, rows, C = q_lat.shape
    t_new = kv_new.shape[1]
    n_pages = page_table.shape[1]
    keys = _PAGES_PER_CHUNK * PAGE_SIZE
    page_table = jnp.concatenate([page_table, page_table[:1]], axis=0)
    grid_spec = pltpu.PrefetchScalarGridSpec(
        num_scalar_prefetch=1,
        grid=(1,),
        in_specs=[
            _const_spec(q_lat.shape),
            _const_spec(q_pe.shape),
            _const_spec(kv_new.shape),
            _const_spec(pe_new.shape),
            pl.BlockSpec(memory_space=pl.ANY),
            pl.BlockSpec(memory_space=pl.ANY),
        ],
        out_specs=pl.BlockSpec((DB, rows, C), lambda i, pt: (0, 0, 0)),
        scratch_shapes=[
            pltpu.VMEM((_DMA_SLOTS, keys, C), F32),
            pltpu.VMEM((_DMA_SLOTS, ROPE_DIM, keys), F32),
            pltpu.VMEM((keys, C), BF16),
            pltpu.VMEM((keys, C), BF16),
            pltpu.VMEM((rows, keys), F32),
            pltpu.VMEM((rows, keys), F32),
            pltpu.VMEM((C, keys), BF16),
            pltpu.VMEM((rows, 1), F32),
            pltpu.VMEM((rows, 1), F32),
            pltpu.VMEM((rows, C), F32),
            pltpu.SemaphoreType.DMA((2, _DMA_SLOTS)),
        ],
    )
    return pl.pallas_call(
        functools.partial(_sattn_kernel, n_pages=n_pages, t_new=t_new),
        out_shape=jax.ShapeDtypeStruct((DB, rows, C), BF16),
        grid_spec=grid_spec,
        compiler_params=_params(1),
        name="sattn",
    )(page_table, q_lat, q_pe, kv_new, pe_new, cache_kv, cache_pe_t)


def _rope_tables(pos):
    half = ROPE_DIM // 2
    freqs = ROPE_THETA ** (-2.0 * jnp.arange(half, dtype=F32) / ROPE_DIM)
    ang = pos.astype(F32)[:, None] * freqs[None, :]
    cos = jnp.cos(ang)
    sin = jnp.sin(ang)
    cos2 = jnp.tile(jnp.concatenate([cos, cos], axis=1), (1, N_HEADS))
    sin2 = jnp.tile(jnp.concatenate([-sin, sin], axis=1), (1, N_HEADS))
    return cos2, sin2


def _pair_pad(w, axis):
    z = jnp.zeros_like(w)
    even = jnp.concatenate([w, z], axis=axis)
    odd = jnp.concatenate([z, w], axis=axis)
    sel = (jnp.arange(w.shape[0]) % 2 == 0).reshape(-1, 1, 1)
    return jnp.where(sel, even, odd)


def kernel(x_prompt, x_sample, cache_kv, cache_pe, page_table, c_prompt, c_sample, w_ada, b_ada, g_ffn1, w1_gate, w1_up, w1_down, g_mix, w_in, g_q, w_uq, g_kv, w_uk, w_uv, ln_v_g, ln_v_b, w_s, b_s, w_pa, w_pb, w_o, g_ffn2, w2_gate, w2_up, w2_down, g_final):
    B, S, D = x_prompt.shape
    DB, T, _ = x_sample.shape
    depth = w_ada.shape[0]
    assert depth == 1
    l = 0
    R = DB * T

    row = lambda v: v.reshape(1, -1)
    wi = w_in[l]
    seg = lambda a, n: wi[:, a:a + n]
    o_ckv = Q_LORA
    o_kpe = o_ckv + KV_LORA
    o_u = o_kpe + ROPE_DIM
    o_v = o_u + GV
    o_ga = o_v + GV
    o_gb = o_ga + D_MODEL
    w_in_r = jnp.concatenate([
        seg(0, Q_LORA), seg(o_ckv, KV_LORA), jnp.tile(seg(o_kpe, ROPE_DIM), (1, N_HEADS)),
        seg(o_u, GV), seg(o_v, GV), seg(o_ga, D_MODEL), seg(o_gb, D_MODEL)], axis=1).astype(BF16)
    wq = w_uq[l].reshape(Q_LORA, N_HEADS, NOPE_DIM + ROPE_DIM)
    w_uq_r = jnp.concatenate([wq[:, :, :NOPE_DIM].reshape(Q_LORA, -1),
                              wq[:, :, NOPE_DIM:].reshape(Q_LORA, -1)], axis=1).astype(BF16)
    w_uk_p = _pair_pad(jnp.transpose(w_uk[l], (1, 2, 0)), 1).astype(BF16)
    w_uv_p = _pair_pad(jnp.transpose(w_uv[l], (1, 0, 2)), 2).astype(BF16)
    bf = lambda w: w[l].astype(BF16)
    w1g, w1u, w1d = bf(w1_gate), bf(w1_up), bf(w1_down)
    w2g, w2u, w2d = bf(w2_gate), bf(w2_up), bf(w2_down)
    wpa, wpb, wo = bf(w_pa), bf(w_pb), bf(w_o)
    gf = row(g_final)

    w_mix_p = w_s[l]
    b_mix_p = jnp.repeat(b_s[l].T, GROUP_DIM, axis=1)
    reps = CHUNK // T
    w_mix_s = jnp.tile(w_s[l][:, :T, :T], (1, reps, reps))
    b_mix_s = jnp.repeat(jnp.tile(b_s[l][:, :T].T, (reps, 1)), GROUP_DIM, axis=1)

    mod = _ada(jnp.concatenate([c_prompt, c_sample], axis=0), w_ada[l], b_ada[l])
    mod_p = mod[:B].reshape(B, N_SUB, 3, D)
    mod_s = jnp.transpose(mod[B:].reshape(DB, N_SUB, 3, D), (1, 2, 0, 3))

    past_len = page_table.shape[1] * PAGE_SIZE
    cos_p, sin_p = _rope_tables(jnp.arange(S))
    cos_s, sin_s = _rope_tables(past_len + jnp.arange(R) % T)

    def mixin(x1, mod, rows_per_seq, tm, chunk_len, cos2, sin2, w_mix, b_mix, want_gv):
        return _mixin(x1, mod, row(g_mix[l]), w_in_r, row(g_q[l]), w_uq_r, w_uk_p, row(g_kv[l]),
                      cos2, sin2, row(ln_v_g[l]), row(ln_v_b[l]), w_mix, b_mix, wpb,
                      rows_per_seq=rows_per_seq, chunk_len=chunk_len, want_gv=want_gv, tm=tm)

    def merge_ffn2(x1, mod_mix, mod_ffn, o_lat, sa, mb, rows_per_seq, tm):
        return _ffn(x1, mod_ffn, row(g_ffn2[l]), w2g, w2u, w2d, gf,
                    rows_per_seq=rows_per_seq, final_norm=True, tm=tm,
                    merge_args=(o_lat, sa, mb, mod_mix, w_uv_p, wpa, wo))

    tm_p = 512
    x1_s = _ffn(x_sample.reshape(1, R, D), mod_s[0], row(g_ffn1[l]), w1g, w1u, w1d, gf,
                rows_per_seq=T, final_norm=False, tm=R)
    kv_s, pe_s, _, _, q_lat_t, q_pe_t, sa_s, mb_s, gv_s = mixin(x1_s, mod_s[1], T, R, T, cos_s, sin_s,
                                                                w_mix_s, b_mix_s, True)
    ql = jnp.transpose(q_lat_t.reshape(N_HEADS, KV_LORA, DB, T), (2, 0, 3, 1)).reshape(DB, N_HEADS * T, KV_LORA)
    qp = jnp.transpose(q_pe_t.reshape(N_HEADS, ROPE_DIM, DB, T), (2, 0, 3, 1)).reshape(DB, N_HEADS * T, ROPE_DIM)

    cache_pe_t = jnp.swapaxes(cache_pe[l], 1, 2)
    o_s = _sattn(page_table, ql, qp, kv_s.reshape(DB, T, KV_LORA), pe_s.reshape(DB, T, ROPE_DIM),
                 cache_kv[l], cache_pe_t)

    x1_p = _ffn(x_prompt, mod_p[:, 0], row(g_ffn1[l]), w1g, w1u, w1d, gf,
                rows_per_seq=0, final_norm=False, tm=tm_p)
    kv_p, pe_p, kcat, kv_t, q_lat_t, q_pe_t, sa_p, mb_p = mixin(x1_p, mod_p[:, 1], 0, tm_p, CHUNK, cos_p, sin_p,
                                                                w_mix_p, b_mix_p, False)
    o_lat_p = _pattn(q_lat_t, q_pe_t, kcat, kv_t)
    y_p = merge_ffn2(x1_p, mod_p[:, 1], mod_p[:, 2], o_lat_p, sa_p, mb_p, 0, tm_p)

    o_lat_s = jnp.transpose(o_s.reshape(DB, N_HEADS, T, KV_LORA), (1, 0, 2, 3)).reshape(1, N_HEADS, R, KV_LORA)
    y_s = merge_ffn2(x1_s, mod_s[1], mod_s[2], o_lat_s, sa_s, mb_s, T, R)

    return (y_p, y_s.reshape(DB, T, D),
            kv_p.reshape(1, B, S, KV_LORA), pe_p.reshape(1, B, S, ROPE_DIM),
            kv_s.reshape(1, DB, T, KV_LORA), pe_s.reshape(1, DB, T, ROPE_DIM),
            gv_s.reshape(1, DB, T, GV))
```

```python
import functools

import jax
import jax.numpy as jnp
from jax import lax
from jax.experimental import pallas as pl
from jax.experimental.pallas import tpu as pltpu

D_MODEL = 1024
N_HEADS = 8
Q_LORA = 256
KV_LORA = 256
NOPE_DIM = 64
ROPE_DIM = 32
V_DIM = 64
ROPE_THETA = 10000.0
SM_SCALE = (NOPE_DIM + ROPE_DIM) ** -0.5
Q_SCALE = SM_SCALE * 1.4426950408889634
CHUNK = 128
GV = 512
N_GROUPS = 8
GROUP_DIM = GV // N_GROUPS
D_FF = 2816
N_SUB = 3
EPS = 1e-6
PAGE_SIZE = 128
PE_WIDE = N_HEADS * ROPE_DIM
KEY_CHUNK = 256
Q_TILE = 256

VMEM_LIMIT_BYTES = 56 * 1024 * 1024
LANES = 128

BF16 = jnp.bfloat16
F32 = jnp.float32

_OFF_CQ = 0
_OFF_CKV = _OFF_CQ + Q_LORA
_OFF_KPE = _OFF_CKV + KV_LORA
_OFF_U = _OFF_KPE + PE_WIDE
_OFF_V = _OFF_U + GV
_OFF_GA = _OFF_V + GV
_OFF_GB = _OFF_GA + D_MODEL
_N_IN_R = _OFF_GB + D_MODEL


def _dot(a, b):
    return jnp.dot(a, b, preferred_element_type=F32)


def _dot_nt(a, b):
    return lax.dot_general(a, b, (((1,), (1,)), ((), ())), preferred_element_type=F32)


def _rms(x, g):
    return x * lax.rsqrt(jnp.mean(x * x, axis=-1, keepdims=True) + EPS) * g


def _modulated(x, g, scale, shift):
    return x * lax.rsqrt(jnp.mean(x * x, axis=-1, keepdims=True) + EPS) * (g * (1.0 + scale)) + shift


def _row_expander(tm, n_seq, rows_per_seq, row0=0):
    if not rows_per_seq:
        return None
    r = lax.broadcasted_iota(jnp.int32, (tm, n_seq), 0) + row0
    s = lax.broadcasted_iota(jnp.int32, (tm, n_seq), 1)
    return (r // rows_per_seq == s).astype(F32)


def _mod_row(mod_ref, k, expand):
    if expand is None:
        return mod_ref[k:k + 1, :]
    return jnp.dot(expand, mod_ref[k], precision=lax.Precision.HIGHEST, preferred_element_type=F32)


def _const_spec(shape):
    nd = len(shape)
    return pl.BlockSpec(shape, lambda *_: (0,) * nd, pipeline_mode=pl.Buffered(1))


def _params(n_grid, flags=None):
    return pltpu.CompilerParams(
        dimension_semantics=("arbitrary",) * n_grid,
        vmem_limit_bytes=VMEM_LIMIT_BYTES,
        flags=flags,
    )


def _ada_kernel(c_ref, w_ref, b_ref, o_ref):
    c = c_ref[...]
    a = (c * jax.nn.sigmoid(c)).astype(BF16)
    o_ref[...] = _dot(a, w_ref[...].astype(BF16)) + b_ref[...]


def _ada(c_all, w_ada, b_ada):
    rows, d = c_all.shape
    n = w_ada.shape[1]
    tn = 1024
    return pl.pallas_call(
        _ada_kernel,
        out_shape=jax.ShapeDtypeStruct((rows, n), F32),
        grid=(n // tn,),
        in_specs=[
            pl.BlockSpec((rows, d), lambda j: (0, 0)),
            pl.BlockSpec((d, tn), lambda j: (0, j)),
            pl.BlockSpec((1, tn), lambda j: (0, j)),
        ],
        out_specs=pl.BlockSpec((rows, tn), lambda j: (0, j)),
        compiler_params=_params(1),
        name="ada",
    )(c_all, w_ada, b_ada.reshape(1, n))


_FF_CHUNKS = ((0, 1536), (1536, 1280))


def _merge_mixers(x, ol_ref, sa_ref, mb_ref, gate, wuv_ref, wpa_ref, wo_ref):
    parts = []
    for jp in range(N_HEADS // 2):
        a = _dot(ol_ref[2 * jp], wuv_ref[2 * jp]) + _dot(ol_ref[2 * jp + 1], wuv_ref[2 * jp + 1])
        parts.append(a.astype(BF16))
    o_a = jnp.concatenate(parts, axis=1)
    merged = sa_ref[...].astype(F32) * _dot(o_a, wpa_ref[...]) + mb_ref[...].astype(F32)
    return x + gate * _dot(merged.astype(BF16), wo_ref[...])


def _ffn_stages(x_fn, mod_ref, g_ref, wg_ref, wu_ref, wd_ref, gf_ref, o_ref, expand, final_norm):
    st = {}

    def norm():
        x = x_fn()
        shift = _mod_row(mod_ref, 0, expand)
        scale = _mod_row(mod_ref, 1, expand)
        st["x"] = x
        st["gate"] = _mod_row(mod_ref, 2, expand)
        st["nb"] = _modulated(x, g_ref[...], scale, shift).astype(BF16)

    def gate_proj(c):
        off, width = _FF_CHUNKS[c]
        st["hg"] = _dot(st["nb"], wg_ref[:, off:off + width])

    def up_proj(c):
        off, width = _FF_CHUNKS[c]
        hg = st.pop("hg")
        st["h"] = (hg * jax.nn.sigmoid(hg) * _dot(st["nb"], wu_ref[:, off:off + width])).astype(BF16)

    def down_proj(c):
        off, width = _FF_CHUNKS[c]
        d = _dot(st.pop("h"), wd_ref[off:off + width, :])
        st["acc"] = d if c == 0 else st["acc"] + d

    def final():
        y = st["x"] + 0.5 * st["gate"] * st["acc"]
        if final_norm:
            y = _rms(y, gf_ref[...])
        o_ref[...] = y

    stages = [norm]
    for c in range(len(_FF_CHUNKS)):
        stages += [functools.partial(gate_proj, c), functools.partial(up_proj, c), functools.partial(down_proj, c)]
    return stages + [final]


def _ffn_kernel(*refs, rows_per_seq, final_norm, merge):
    if merge:
        (x_ref, ol_ref, sa_ref, mb_ref, modm_ref, wuv_ref, wpa_ref, wo_ref,
         mod_ref, g_ref, wg_ref, wu_ref, wd_ref, gf_ref, o_ref) = refs
    else:
        x_ref, mod_ref, g_ref, wg_ref, wu_ref, wd_ref, gf_ref, o_ref = refs
    expand = _row_expander(x_ref.shape[0], mod_ref.shape[1], rows_per_seq)

    def x_fn():
        x = x_ref[...]
        if merge:
            x = _merge_mixers(x, ol_ref, sa_ref, mb_ref, _mod_row(modm_ref, 2, expand), wuv_ref, wpa_ref, wo_ref)
        return x

    for stage in _ffn_stages(x_fn, mod_ref, g_ref, wg_ref, wu_ref, wd_ref, gf_ref, o_ref, expand, final_norm):
        stage()


def _mod_spec(mod, rows_per_seq):
    if rows_per_seq:
        return pl.BlockSpec(mod.shape, lambda g, i, *_: (0, 0, 0))
    return pl.BlockSpec((None, 3, D_MODEL), lambda g, i, *_: (g, 0, 0))


def _row_spec(tm, width):
    return pl.BlockSpec((None, tm, width), lambda g, i, *_: (g, i, 0))


def _check_tile(x, tm, rows_per_seq):
    G, T, _ = x.shape
    assert T % tm == 0
    if rows_per_seq:
        assert G == 1 and tm == T, "per-sequence modulation is expanded for one tile holding every row"


def _ffn(x, mod, g, wg, wu, wd, g_final, *, rows_per_seq, final_norm, tm, merge_args=None):
    G, T, _ = x.shape
    _check_tile(x, tm, rows_per_seq)
    operands = [x]
    in_specs = [_row_spec(tm, D_MODEL)]
    if merge_args is not None:
        o_lat, sa, mb, mod_mix, w_uv_p, w_pa, w_o = merge_args
        operands += [o_lat, sa, mb, mod_mix, w_uv_p, w_pa, w_o]
        in_specs += [
            pl.BlockSpec((None, N_HEADS, tm, KV_LORA), lambda g, i: (g, 0, i, 0)),
            _row_spec(tm, D_MODEL),
            _row_spec(tm, D_MODEL),
            _mod_spec(mod_mix, rows_per_seq),
            _const_spec(w_uv_p.shape),
            _const_spec(w_pa.shape),
            _const_spec(w_o.shape),
        ]
    operands += [mod, g, wg, wu, wd, g_final]
    in_specs += [
        _mod_spec(mod, rows_per_seq),
        _const_spec((1, D_MODEL)),
        _const_spec(wg.shape),
        _const_spec(wu.shape),
        _const_spec(wd.shape),
        _const_spec((1, D_MODEL)),
    ]
    return pl.pallas_call(
        functools.partial(_ffn_kernel, rows_per_seq=rows_per_seq, final_norm=final_norm,
                          merge=merge_args is not None),
        out_shape=jax.ShapeDtypeStruct(x.shape, F32),
        grid=(G, T // tm),
        in_specs=in_specs,
        out_specs=_row_spec(tm, D_MODEL),
        compiler_params=_params(2),
        name="merge_ffn" if merge_args is not None else "ffn",
    )(*operands)


_MIXIN_SUB_ROWS = 512


def _rope(x, cos2, sin2):
    halves = []
    for k in range(x.shape[1] // LANES):
        xs = x[:, k * LANES:(k + 1) * LANES]
        lane = lax.broadcasted_iota(jnp.int32, xs.shape, 1)
        first = (lane % ROPE_DIM) < (ROPE_DIM // 2)
        swapped = jnp.where(first, pltpu.roll(xs, LANES - ROPE_DIM // 2, 1), pltpu.roll(xs, ROPE_DIM // 2, 1))
        halves.append(swapped)
    swapped = jnp.concatenate(halves, axis=1)
    return x * cos2 + swapped * sin2


def _mixin_kernel(x_ref, mod_ref, g_ref, win_ref, gq_ref, wuq_ref, wuk_ref, gkv_ref, cos_ref, sin_ref,
                  lng_ref, lnb_ref, wmix_ref, bmix_ref, wpb_ref,
                  kv_ref, pe_ref, kcat_ref, kvt_ref, qlat_ref, qpe_ref, sa_ref, mb_ref, gv_ref,
                  *, rows_per_seq, chunk_len):
    tm = x_ref.shape[0]
    row = lax.broadcasted_iota(jnp.int32, (CHUNK, CHUNK), 0)
    col = lax.broadcasted_iota(jnp.int32, (CHUNK, CHUNK), 1)
    keep = (col <= row) & ((row // chunk_len) == (col // chunk_len))
    wm = [jnp.where(keep, wmix_ref[g], 0.0).astype(BF16) for g in range(N_GROUPS)]
    lane = lax.broadcasted_iota(jnp.int32, (CHUNK, LANES), 1)
    low = lane < GROUP_DIM
    bias = bmix_ref[...]

    sub = _MIXIN_SUB_ROWS
    for r in range(tm // sub):
        rows = slice(r * sub, (r + 1) * sub)
        x = x_ref[rows, :]
        expand = _row_expander(sub, mod_ref.shape[1], rows_per_seq, r * sub)
        shift = _mod_row(mod_ref, 0, expand)
        scale = _mod_row(mod_ref, 1, expand)
        nb = _modulated(x, g_ref[...], scale, shift).astype(BF16)
        cos2 = cos_ref[rows, :]
        sin2 = sin_ref[rows, :]

        def proj(off, width, nb=nb):
            return _dot(nb, win_ref[:, off:off + width])

        cqn = _rms(proj(_OFF_CQ, Q_LORA), gq_ref[...]).astype(BF16)
        q = _dot(cqn, wuq_ref[...])
        for h in range(N_HEADS):
            pair = q[:, (h // 2) * LANES:(h // 2 + 1) * LANES].astype(BF16)
            qlat_h = _dot(pair, wuk_ref[h]) * Q_SCALE
            qlat_t = qlat_h.T.astype(BF16)
            for t in range(sub // Q_TILE):
                qlat_ref[r * (sub // Q_TILE) + t, h] = qlat_t[:, t * Q_TILE:(t + 1) * Q_TILE]
        qpe_t = (_rope(q[:, N_HEADS * NOPE_DIM:], cos2, sin2) * Q_SCALE).T.astype(BF16)
        for t in range(sub // Q_TILE):
            qpe_ref[r * (sub // Q_TILE) + t] = qpe_t[:, t * Q_TILE:(t + 1) * Q_TILE]

        kv = _rms(proj(_OFF_CKV, KV_LORA), gkv_ref[...])
        kv_ref[rows, :] = kv
        kpe = _rope(proj(_OFF_KPE, PE_WIDE), cos2, sin2)
        pe_ref[rows, :] = kpe[:, :ROPE_DIM]
        kcat_ref[rows, :] = jnp.concatenate([kv.astype(BF16), kpe.astype(BF16)], axis=1)
        kv_t = kv.T.astype(BF16)
        for c in range(sub // KEY_CHUNK):
            kvt_ref[r * (sub // KEY_CHUNK) + c] = kv_t[:, c * KEY_CHUNK:(c + 1) * KEY_CHUNK]

        u = jax.nn.gelu(proj(_OFF_U, GV))
        v = jax.nn.gelu(proj(_OFF_V, GV))
        mu = jnp.mean(v, axis=-1, keepdims=True)
        vc = v - mu
        var = jnp.mean(vc * vc, axis=-1, keepdims=True)
        v_n = vc * lax.rsqrt(var + EPS) * lng_ref[...] + lnb_ref[...]
        if gv_ref is not None:
            gv_ref[rows, :] = v_n
        vnb = v_n.astype(BF16)
        tiles = []
        for c in range(sub // CHUNK):
            cols = []
            for jp in range(GV // LANES):
                vv = vnb[c * CHUNK:(c + 1) * CHUNK, jp * LANES:(jp + 1) * LANES]
                cols.append(jnp.where(low, _dot(wm[2 * jp], vv), _dot(wm[2 * jp + 1], vv)))
            tiles.append(jnp.concatenate(cols, axis=1) + bias)
        mixed = jnp.concatenate(tiles, axis=0)
        o_b = (u * mixed).astype(BF16)
        mb_ref[rows, :] = (jax.nn.sigmoid(proj(_OFF_GB, D_MODEL)) * _dot(o_b, wpb_ref[...])).astype(BF16)
        sa_ref[rows, :] = jax.nn.sigmoid(proj(_OFF_GA, D_MODEL)).astype(BF16)


def _mixin(x, mod, g_mix, w_in_r, g_q, w_uq_r, w_uk_p, g_kv, cos2, sin2, ln_g, ln_b, w_mix, b_mix, w_pb,
           *, rows_per_seq, chunk_len, want_gv, tm):
    G, T, _ = x.shape
    _check_tile(x, tm, rows_per_seq)
    grid = (G, T // tm)
    out_shape = [
        jax.ShapeDtypeStruct((G, T, KV_LORA), F32),
        jax.ShapeDtypeStruct((G, T, ROPE_DIM), F32),
        jax.ShapeDtypeStruct((G, T, KV_LORA + PE_WIDE), BF16),
        jax.ShapeDtypeStruct((G, T // KEY_CHUNK, KV_LORA, KEY_CHUNK), BF16),
        jax.ShapeDtypeStruct((G, T // Q_TILE, N_HEADS, KV_LORA, Q_TILE), BF16),
        jax.ShapeDtypeStruct((G, T // Q_TILE, PE_WIDE, Q_TILE), BF16),
        jax.ShapeDtypeStruct((G, T, D_MODEL), BF16),
        jax.ShapeDtypeStruct((G, T, D_MODEL), BF16),
    ]
    out_specs = [
        _row_spec(tm, KV_LORA),
        _row_spec(tm, ROPE_DIM),
        _row_spec(tm, KV_LORA + PE_WIDE),
        pl.BlockSpec((None, tm // KEY_CHUNK, KV_LORA, KEY_CHUNK), lambda g, i: (g, i, 0, 0)),
        pl.BlockSpec((None, tm // Q_TILE, N_HEADS, KV_LORA, Q_TILE), lambda g, i: (g, i, 0, 0, 0)),
        pl.BlockSpec((None, tm // Q_TILE, PE_WIDE, Q_TILE), lambda g, i: (g, i, 0, 0)),
        _row_spec(tm, D_MODEL),
        _row_spec(tm, D_MODEL),
    ]
    if want_gv:
        out_shape.append(jax.ShapeDtypeStruct((G, T, GV), F32))
        out_specs.append(_row_spec(tm, GV))
        body = functools.partial(_mixin_kernel, rows_per_seq=rows_per_seq, chunk_len=chunk_len)
    else:
        def body(*refs):
            _mixin_kernel(*refs, None, rows_per_seq=rows_per_seq, chunk_len=chunk_len)
    return pl.pallas_call(
        body,
        out_shape=out_shape,
        grid=grid,
        in_specs=[
            _row_spec(tm, D_MODEL),
            _mod_spec(mod, rows_per_seq),
            _const_spec((1, D_MODEL)),
            _const_spec(w_in_r.shape),
            _const_spec((1, Q_LORA)),
            _const_spec(w_uq_r.shape),
            _const_spec(w_uk_p.shape),
            _const_spec((1, KV_LORA)),
            pl.BlockSpec((tm, PE_WIDE), lambda g, i: (i, 0)),
            pl.BlockSpec((tm, PE_WIDE), lambda g, i: (i, 0)),
            _const_spec((1, GV)),
            _const_spec((1, GV)),
            _const_spec(w_mix.shape),
            _const_spec(b_mix.shape),
            _const_spec(w_pb.shape),
        ],
        out_specs=out_specs,
        compiler_params=_params(2),
        name="mixin",
    )(x, mod, g_mix, w_in_r, g_q, w_uq_r, w_uk_p, g_kv, cos2, sin2, ln_g, ln_b, w_mix, b_mix, w_pb)


_TQ = Q_TILE
_COL_BLOCK = 256


def _pattn_kernel(qlat_ref, qpe_ref, kcat_ref, kvt_ref, o_ref, qt_sc, s0_sc, s1_sc, m_sc, l_sc, acc_sc):
    i = pl.program_id(1)
    tk = KEY_CHUNK
    head_of_row = lax.broadcasted_iota(jnp.int32, (PE_WIDE, _TQ), 0) // ROPE_DIM
    qp = qpe_ref[...]
    for h in range(N_HEADS):
        qt_sc[0:KV_LORA, h * _TQ:(h + 1) * _TQ] = qlat_ref[h]
        qt_sc[KV_LORA:, h * _TQ:(h + 1) * _TQ] = jnp.where(head_of_row == h, qp, jnp.zeros_like(qp))
    m_sc[...] = jnp.full(m_sc.shape, -1e30, F32)
    l_sc[...] = jnp.zeros(l_sc.shape, F32)
    acc_sc[...] = jnp.zeros(acc_sc.shape, F32)

    def scores(j, s_ref):
        k = kcat_ref[pl.ds(pl.multiple_of(j * tk, tk), tk), :]
        s_ref[...] = _dot(k, qt_sc[...])

    def accumulate(j, s_ref, masked):
        v_t = kvt_ref[j]
        for cb in range(N_HEADS * _TQ // _COL_BLOCK):
            cols = slice(cb * _COL_BLOCK, (cb + 1) * _COL_BLOCK)
            s = s_ref[:, cols]
            if masked:
                col = cb * _COL_BLOCK + lax.broadcasted_iota(jnp.int32, s.shape, 1)
                q_pos = i * _TQ + col % _TQ
                k_pos = j * tk + lax.broadcasted_iota(jnp.int32, s.shape, 0)
                s = jnp.where(k_pos <= q_pos, s, -jnp.inf)
            m_prev = m_sc[:, cols]
            m_new = jnp.maximum(m_prev, jnp.max(s, axis=0, keepdims=True))
            alpha = jnp.exp2(m_prev - m_new)
            p = jnp.exp2(s - m_new)
            l_sc[:, cols] = alpha * l_sc[:, cols] + jnp.sum(p, axis=0, keepdims=True)
            acc_sc[:, cols] = alpha * acc_sc[:, cols] + _dot(v_t, p.astype(BF16))
            m_sc[:, cols] = m_new

    n_full = (i * _TQ) // tk
    scores(0, s0_sc)

    def pair(p, carry):
        j = 2 * p
        scores(j + 1, s1_sc)
        accumulate(j, s0_sc, False)
        scores(j + 2, s0_sc)
        accumulate(j + 1, s1_sc, False)
        return carry

    lax.fori_loop(0, n_full // 2, pair, 0)

    assert _TQ <= tk, "only chunk n_full may hold masked keys"

    @pl.when(n_full % 2 == 0)
    def _():
        accumulate(n_full, s0_sc, True)

    @pl.when(n_full % 2 == 1)
    def _():
        scores(n_full, s1_sc)
        accumulate(n_full - 1, s0_sc, False)
        accumulate(n_full, s1_sc, True)

    o_t = acc_sc[...] / l_sc[...]
    for h in range(N_HEADS):
        o_ref[h] = o_t[:, h * _TQ:(h + 1) * _TQ].T.astype(BF16)


def _pattn(q_lat_t, q_pe_t, kcat, kv_t):
    B, n_tiles, H, C, _ = q_lat_t.shape
    S = n_tiles * _TQ
    cols = H * _TQ
    return pl.pallas_call(
        _pattn_kernel,
        out_shape=jax.ShapeDtypeStruct((B, H, S, C), BF16),
        grid=(B, S // _TQ),
        in_specs=[
            pl.BlockSpec((None, None, H, C, _TQ), lambda b, i: (b, i, 0, 0, 0)),
            pl.BlockSpec((None, None, PE_WIDE, _TQ), lambda b, i: (b, i, 0, 0)),
            pl.BlockSpec((None, S, C + PE_WIDE), lambda b, i: (b, 0, 0)),
            pl.BlockSpec((None, S // KEY_CHUNK, C, KEY_CHUNK), lambda b, i: (b, 0, 0, 0)),
        ],
        out_specs=pl.BlockSpec((None, H, _TQ, C), lambda b, i: (b, 0, i, 0)),
        scratch_shapes=[
            pltpu.VMEM((C + PE_WIDE, cols), BF16),
            pltpu.VMEM((KEY_CHUNK, cols), F32),
            pltpu.VMEM((KEY_CHUNK, cols), F32),
            pltpu.VMEM((1, cols), F32),
            pltpu.VMEM((1, cols), F32),
            pltpu.VMEM((C, cols), F32),
        ],
        compiler_params=_params(2),
        name="pattn",
    )(q_lat_t, q_pe_t, kcat, kv_t)


_PAGES_PER_CHUNK = 32


_DMA_SLOTS = 3


def _sattn_kernel(pt_ref, ql_ref, qp_ref, kvn_ref, pen_ref, ckv_hbm, cpet_hbm, o_ref,
                  kvbuf, pebuf, kvb16_0, kvb16_1, s0_sc, s1_sc, kt_sc, m_sc, l_sc, acc_sc, sem, *, n_pages, t_new):
    pc = _PAGES_PER_CHUNK
    n_chunks = n_pages // pc
    n_seq, rows, _ = ql_ref.shape
    total = n_seq * n_chunks
    assert n_chunks % 2 == 0 and total > _DMA_SLOTS
    work = ((kvb16_0, s0_sc), (kvb16_1, s1_sc))

    def copies(g, slot):
        b = g // n_chunks
        c = g % n_chunks
        out = []
        for p in range(pc):
            page = pt_ref[b, c * pc + p]
            dst = pl.ds(p * PAGE_SIZE, PAGE_SIZE)
            out.append(pltpu.make_async_copy(ckv_hbm.at[page], kvbuf.at[slot, dst], sem.at[0, slot]))
            out.append(pltpu.make_async_copy(cpet_hbm.at[page], pebuf.at[slot, :, dst], sem.at[1, slot]))
        return out

    def start(g):
        for cp in copies(g, g % _DMA_SLOTS):
            cp.start()

    def wait(g):
        for cp in copies(g, g % _DMA_SLOTS):
            cp.wait()

    def scores(g, w):
        kvb16, s_sc = work[w]
        b = g // n_chunks
        slot = g % _DMA_SLOTS
        wait(g)
        kvb = kvbuf[slot].astype(BF16)
        kvb16[...] = kvb
        peb = pebuf[slot].astype(BF16)
        kt_sc[...] = kvb.T
        s_sc[...] = _dot(ql_ref[b], kt_sc[...]) + _dot(qp_ref[b], peb)

    def reset_state():
        m_sc[...] = jnp.full(m_sc.shape, -1e30, F32)
        l_sc[...] = jnp.zeros(l_sc.shape, F32)
        acc_sc[...] = jnp.zeros(acc_sc.shape, F32)

    def accumulate(w):
        kvb16, s_sc = work[w]
        s = s_sc[...]
        m_prev = m_sc[...]
        m_new = jnp.maximum(m_prev, jnp.max(s, axis=-1, keepdims=True))
        alpha = jnp.exp2(m_prev - m_new)
        p = jnp.exp2(s - m_new)
        l_sc[...] = alpha * l_sc[...] + jnp.sum(p, axis=-1, keepdims=True)
        acc_sc[...] = alpha * acc_sc[...] + _dot(p.astype(BF16), kvb16[...])
        m_sc[...] = m_new

    def finish(b):
        qlf = ql_ref[b].astype(F32)
        qpf = qp_ref[b].astype(F32)
        kvn = kvn_ref[b]
        pen = pen_ref[b]
        t_of_row = lax.broadcasted_iota(jnp.int32, (rows, 1), 0) % t_new
        s_new = []
        for t in range(t_new):
            st = (jnp.sum(qlf * kvn[t:t + 1, :], axis=-1, keepdims=True)
                  + jnp.sum(qpf * pen[t:t + 1, :], axis=-1, keepdims=True))
            s_new.append(jnp.where(t <= t_of_row, st, -jnp.inf))
        m_prev = m_sc[...]
        m_new = m_prev
        for st in s_new:
            m_new = jnp.maximum(m_new, st)
        alpha = jnp.exp2(m_prev - m_new)
        l_new = alpha * l_sc[...]
        acc = alpha * acc_sc[...]
        for t, st in enumerate(s_new):
            pt = jnp.exp2(st - m_new)
            l_new = l_new + pt
            acc = acc + pt * kvn[t:t + 1, :]
        o_ref[b] = (acc / l_new).astype(o_ref.dtype)
        reset_state()

    reset_state()
    for g0 in range(_DMA_SLOTS):
        start(g0)
    scores(0, 0)

    def body(k, carry):
        g = 2 * k + 1
        start(g + _DMA_SLOTS - 1)
        scores(g, 1)
        accumulate(0)
        start(g + _DMA_SLOTS)
        scores(g + 1, 0)
        accumulate(1)

        @pl.when((g + 1) % n_chunks == 0)
        def _():
            finish((g + 1) // n_chunks - 1)

        return carry

    lax.fori_loop(0, total // 2 - 1 + pl.program_id(0), body, 0)
    scores(total - 1, 1)
    accumulate(0)
    accumulate(1)
    finish(n_seq - 1)
    wait(total)


def _sattn(page_table, q_lat, q_pe, kv_new, pe_new, cache_kv, cache_pe_t):
    DB, rows, C = q_lat.shape
    t_new = kv_new.shape[1]
    n_pages = page_table.shape[1]
    keys = _PAGES_PER_CHUNK * PAGE_SIZE
    page_table = jnp.concatenate([page_table, page_table[:1]], axis=0)
    grid_spec = pltpu.PrefetchScalarGridSpec(
        num_scalar_prefetch=1,
        grid=(1,),
        in_specs=[
            _const_spec(q_lat.shape),
            _const_spec(q_pe.shape),
            _const_spec(kv_new.shape),
            _const_spec(pe_new.shape),
            pl.BlockSpec(memory_space=pl.ANY),
            pl.BlockSpec(memory_space=pl.ANY),
        ],
        out_specs=pl.BlockSpec((DB, rows, C), lambda i, pt: (0, 0, 0)),
        scratch_shapes=[
            pltpu.VMEM((_DMA_SLOTS, keys, C), F32),
            pltpu.VMEM((_DMA_SLOTS, ROPE_DIM, keys), F32),
            pltpu.VMEM((keys, C), BF16),
            pltpu.VMEM((keys, C), BF16),
            pltpu.VMEM((rows, keys), F32),
            pltpu.VMEM((rows, keys), F32),
            pltpu.VMEM((C, keys), BF16),
            pltpu.VMEM((rows, 1), F32),
            pltpu.VMEM((rows, 1), F32),
            pltpu.VMEM((rows, C), F32),
            pltpu.SemaphoreType.DMA((2, _DMA_SLOTS)),
        ],
    )
    return pl.pallas_call(
        functools.partial(_sattn_kernel, n_pages=n_pages, t_new=t_new),
        out_shape=jax.ShapeDtypeStruct((DB, rows, C), BF16),
        grid_spec=grid_spec,
        compiler_params=_params(1),
        name="sattn",
    )(page_table, q_lat, q_pe, kv_new, pe_new, cache_kv, cache_pe_t)


def _rope_tables(pos):
    half = ROPE_DIM // 2
    freqs = ROPE_THETA ** (-2.0 * jnp.arange(half, dtype=F32) / ROPE_DIM)
    ang = pos.astype(F32)[:, None] * freqs[None, :]
    cos = jnp.cos(ang)
    sin = jnp.sin(ang)
    cos2 = jnp.tile(jnp.concatenate([cos, cos], axis=1), (1, N_HEADS))
    sin2 = jnp.tile(jnp.concatenate([-sin, sin], axis=1), (1, N_HEADS))
    return cos2, sin2


def _pair_pad(w, axis):
    z = jnp.zeros_like(w)
    even = jnp.concatenate([w, z], axis=axis)
    odd = jnp.concatenate([z, w], axis=axis)
    sel = (jnp.arange(w.shape[0]) % 2 == 0).reshape(-1, 1, 1)
    return jnp.where(sel, even, odd)


def kernel(x_prompt, x_sample, cache_kv, cache_pe, page_table, c_prompt, c_sample, w_ada, b_ada, g_ffn1, w1_gate, w1_up, w1_down, g_mix, w_in, g_q, w_uq, g_kv, w_uk, w_uv, ln_v_g, ln_v_b, w_s, b_s, w_pa, w_pb, w_o, g_ffn2, w2_gate, w2_up, w2_down, g_final):
    B, S, D = x_prompt.shape
    DB, T, _ = x_sample.shape
    depth = w_ada.shape[0]
    assert depth == 1
    l = 0
    R = DB * T

    row = lambda v: v.reshape(1, -1)
    wi = w_in[l]
    seg = lambda a, n: wi[:, a:a + n]
    o_ckv = Q_LORA
    o_kpe = o_ckv + KV_LORA
    o_u = o_kpe + ROPE_DIM
    o_v = o_u + GV
    o_ga = o_v + GV
    o_gb = o_ga + D_MODEL
    w_in_r = jnp.concatenate([
        seg(0, Q_LORA), seg(o_ckv, KV_LORA), jnp.tile(seg(o_kpe, ROPE_DIM), (1, N_HEADS)),
        seg(o_u, GV), seg(o_v, GV), seg(o_ga, D_MODEL), seg(o_gb, D_MODEL)], axis=1).astype(BF16)
    wq = w_uq[l].reshape(Q_LORA, N_HEADS, NOPE_DIM + ROPE_DIM)
    w_uq_r = jnp.concatenate([wq[:, :, :NOPE_DIM].reshape(Q_LORA, -1),
                              wq[:, :, NOPE_DIM:].reshape(Q_LORA, -1)], axis=1).astype(BF16)
    w_uk_p = _pair_pad(jnp.transpose(w_uk[l], (1, 2, 0)), 1).astype(BF16)
    w_uv_p = _pair_pad(jnp.transpose(w_uv[l], (1, 0, 2)), 2).astype(BF16)
    bf = lambda w: w[l].astype(BF16)
    w1g, w1u, w1d = bf(w1_gate), bf(w1_up), bf(w1_down)
    w2g, w2u, w2d = bf(w2_gate), bf(w2_up), bf(w2_down)
    wpa, wpb, wo = bf(w_pa), bf(w_pb), bf(w_o)
    gf = row(g_final)

    w_mix_p = w_s[l]
    b_mix_p = jnp.repeat(b_s[l].T, GROUP_DIM, axis=1)
    reps = CHUNK // T
    w_mix_s = jnp.tile(w_s[l][:, :T, :T], (1, reps, reps))
    b_mix_s = jnp.repeat(jnp.tile(b_s[l][:, :T].T, (reps, 1)), GROUP_DIM, axis=1)

    mod = _ada(jnp.concatenate([c_prompt, c_sample], axis=0), w_ada[l], b_ada[l])
    mod_p = mod[:B].reshape(B, N_SUB, 3, D)
    mod_s = jnp.transpose(mod[B:].reshape(DB, N_SUB, 3, D), (1, 2, 0, 3))

    past_len = page_table.shape[1] * PAGE_SIZE
    cos_p, sin_p = _rope_tables(jnp.arange(S))
    cos_s, sin_s = _rope_tables(past_len + jnp.arange(R) % T)

    def mixin(x1, mod, rows_per_seq, tm, chunk_len, cos2, sin2, w_mix, b_mix, want_gv):
        return _mixin(x1, mod, row(g_mix[l]), w_in_r, row(g_q[l]), w_uq_r, w_uk_p, row(g_kv[l]),
                      cos2, sin2, row(ln_v_g[l]), row(ln_v_b[l]), w_mix, b_mix, wpb,
                      rows_per_seq=rows_per_seq, chunk_len=chunk_len, want_gv=want_gv, tm=tm)

    def merge_ffn2(x1, mod_mix, mod_ffn, o_lat, sa, mb, rows_per_seq, tm):
        return _ffn(x1, mod_ffn, row(g_ffn2[l]), w2g, w2u, w2d, gf,
                    rows_per_seq=rows_per_seq, final_norm=True, tm=tm,
                    merge_args=(o_lat, sa, mb, mod_mix, w_uv_p, wpa, wo))

    tm_p = 512
    x1_s = _ffn(x_sample.reshape(1, R, D), mod_s[0], row(g_ffn1[l]), w1g, w1u, w1d, gf,
                rows_per_seq=T, final_norm=False, tm=R)
    kv_s, pe_s, _, _, q_lat_t, q_pe_t, sa_s, mb_s, gv_s = mixin(x1_s, mod_s[1], T, R, T, cos_s, sin_s,
                                                                w_mix_s, b_mix_s, True)
    ql = jnp.transpose(q_lat_t[0], (0, 3, 1, 2)).reshape(DB, T, N_HEADS, KV_LORA)
    ql = jnp.transpose(ql, (0, 2, 1, 3)).reshape(DB, N_HEADS * T, KV_LORA)
    qp = jnp.transpose(q_pe_t[0], (0, 2, 1)).reshape(DB, T, N_HEADS, ROPE_DIM)
    qp = jnp.transpose(qp, (0, 2, 1, 3)).reshape(DB, N_HEADS * T, ROPE_DIM)

    cache_pe_t = jnp.swapaxes(cache_pe[l], 1, 2)
    o_s = _sattn(page_table, ql, qp, kv_s.reshape(DB, T, KV_LORA), pe_s.reshape(DB, T, ROPE_DIM),
                 cache_kv[l], cache_pe_t)

    x1_p = _ffn(x_prompt, mod_p[:, 0], row(g_ffn1[l]), w1g, w1u, w1d, gf,
                rows_per_seq=0, final_norm=False, tm=tm_p)
    kv_p, pe_p, kcat, kv_t, q_lat_t, q_pe_t, sa_p, mb_p = mixin(x1_p, mod_p[:, 1], 0, tm_p, CHUNK, cos_p, sin_p,
                                                                w_mix_p, b_mix_p, False)
    o_lat_p = _pattn(q_lat_t, q_pe_t, kcat, kv_t)
    y_p = merge_ffn2(x1_p, mod_p[:, 1], mod_p[:, 2], o_lat_p, sa_p, mb_p, 0, tm_p)

    o_lat_s = jnp.transpose(o_s.reshape(DB, N_HEADS, T, KV_LORA), (1, 0, 2, 3)).reshape(1, N_HEADS, R, KV_LORA)
    y_s = merge_ffn2(x1_s, mod_s[1], mod_s[2], o_lat_s, sa_s, mb_s, T, R)

    return (y_p, y_s.reshape(DB, T, D),
            kv_p.reshape(1, B, S, KV_LORA), pe_p.reshape(1, B, S, ROPE_DIM),
            kv_s.reshape(1, DB, T, KV_LORA), pe_s.reshape(1, DB, T, ROPE_DIM),
            gv_s.reshape(1, DB, T, GV))
```

```python
import functools

import jax
import jax.numpy as jnp
from jax import lax
from jax.experimental import pallas as pl
from jax.experimental.pallas import tpu as pltpu

D_MODEL = 1024
N_HEADS = 8
Q_LORA = 256
KV_LORA = 256
NOPE_DIM = 64
ROPE_DIM = 32
V_DIM = 64
ROPE_THETA = 10000.0
SM_SCALE = (NOPE_DIM + ROPE_DIM) ** -0.5
Q_SCALE = SM_SCALE * 1.4426950408889634
CHUNK = 128
GV = 512
N_GROUPS = 8
GROUP_DIM = GV // N_GROUPS
D_FF = 2816
N_SUB = 3
EPS = 1e-6
PAGE_SIZE = 128
PE_WIDE = N_HEADS * ROPE_DIM
KEY_CHUNK = 256
Q_TILE = 256

VMEM_LIMIT_BYTES = 56 * 1024 * 1024
LANES = 128

BF16 = jnp.bfloat16
F32 = jnp.float32

_OFF_CQ = 0
_OFF_CKV = _OFF_CQ + Q_LORA
_OFF_KPE = _OFF_CKV + KV_LORA
_OFF_U = _OFF_KPE + PE_WIDE
_OFF_V = _OFF_U + GV
_OFF_GA = _OFF_V + GV
_OFF_GB = _OFF_GA + D_MODEL
_N_IN_R = _OFF_GB + D_MODEL


def _dot(a, b):
    return jnp.dot(a, b, preferred_element_type=F32)


def _dot_nt(a, b):
    return lax.dot_general(a, b, (((1,), (1,)), ((), ())), preferred_element_type=F32)


def _rms(x, g):
    return x * lax.rsqrt(jnp.mean(x * x, axis=-1, keepdims=True) + EPS) * g


def _modulated(x, g, scale, shift):
    return x * lax.rsqrt(jnp.mean(x * x, axis=-1, keepdims=True) + EPS) * (g * (1.0 + scale)) + shift


def _row_expander(tm, n_seq, rows_per_seq, row0=0):
    if not rows_per_seq:
        return None
    r = lax.broadcasted_iota(jnp.int32, (tm, n_seq), 0) + row0
    s = lax.broadcasted_iota(jnp.int32, (tm, n_seq), 1)
    return (r // rows_per_seq == s).astype(F32)


def _mod_row(mod_ref, k, expand):
    if expand is None:
        return mod_ref[k:k + 1, :]
    return jnp.dot(expand, mod_ref[k], precision=lax.Precision.HIGHEST, preferred_element_type=F32)


def _const_spec(shape):
    nd = len(shape)
    return pl.BlockSpec(shape, lambda *_: (0,) * nd, pipeline_mode=pl.Buffered(1))


def _params(n_grid, flags=None):
    return pltpu.CompilerParams(
        dimension_semantics=("arbitrary",) * n_grid,
        vmem_limit_bytes=VMEM_LIMIT_BYTES,
        flags=flags,
    )


def _ada_kernel(c_ref, w_ref, b_ref, o_ref):
    c = c_ref[...]
    a = (c * jax.nn.sigmoid(c)).astype(BF16)
    o_ref[...] = _dot(a, w_ref[...].astype(BF16)) + b_ref[...]


def _ada(c_all, w_ada, b_ada):
    rows, d = c_all.shape
    n = w_ada.shape[1]
    tn = 1024
    return pl.pallas_call(
        _ada_kernel,
        out_shape=jax.ShapeDtypeStruct((rows, n), F32),
        grid=(n // tn,),
        in_specs=[
            pl.BlockSpec((rows, d), lambda j: (0, 0)),
            pl.BlockSpec((d, tn), lambda j: (0, j)),
            pl.BlockSpec((1, tn), lambda j: (0, j)),
        ],
        out_specs=pl.BlockSpec((rows, tn), lambda j: (0, j)),
        compiler_params=_params(1),
        name="ada",
    )(c_all, w_ada, b_ada.reshape(1, n))


_FF_CHUNKS = ((0, 1536), (1536, 1280))


def _merge_mixers(x, ol_ref, sa_ref, mb_ref, gate, wuv_ref, wpa_ref, wo_ref):
    parts = []
    for jp in range(N_HEADS // 2):
        a = _dot(ol_ref[2 * jp], wuv_ref[2 * jp]) + _dot(ol_ref[2 * jp + 1], wuv_ref[2 * jp + 1])
        parts.append(a.astype(BF16))
    o_a = jnp.concatenate(parts, axis=1)
    merged = sa_ref[...].astype(F32) * _dot(o_a, wpa_ref[...]) + mb_ref[...].astype(F32)
    return x + gate * _dot(merged.astype(BF16), wo_ref[...])


def _ffn_stages(x_fn, mod_ref, g_ref, wg_ref, wu_ref, wd_ref, gf_ref, o_ref, expand, final_norm):
    st = {}

    def norm():
        x = x_fn()
        shift = _mod_row(mod_ref, 0, expand)
        scale = _mod_row(mod_ref, 1, expand)
        st["x"] = x
        st["gate"] = _mod_row(mod_ref, 2, expand)
        st["nb"] = _modulated(x, g_ref[...], scale, shift).astype(BF16)

    def gate_proj(c):
        off, width = _FF_CHUNKS[c]
        st["hg"] = _dot(st["nb"], wg_ref[:, off:off + width])

    def up_proj(c):
        off, width = _FF_CHUNKS[c]
        hg = st.pop("hg")
        st["h"] = (hg * jax.nn.sigmoid(hg) * _dot(st["nb"], wu_ref[:, off:off + width])).astype(BF16)

    def down_proj(c):
        off, width = _FF_CHUNKS[c]
        d = _dot(st.pop("h"), wd_ref[off:off + width, :])
        st["acc"] = d if c == 0 else st["acc"] + d

    def final():
        y = st["x"] + 0.5 * st["gate"] * st["acc"]
        if final_norm:
            y = _rms(y, gf_ref[...])
        o_ref[...] = y

    stages = [norm]
    for c in range(len(_FF_CHUNKS)):
        stages += [functools.partial(gate_proj, c), functools.partial(up_proj, c), functools.partial(down_proj, c)]
    return stages + [final]


def _ffn_kernel(*refs, rows_per_seq, final_norm, merge):
    if merge:
        (x_ref, ol_ref, sa_ref, mb_ref, modm_ref, wuv_ref, wpa_ref, wo_ref,
         mod_ref, g_ref, wg_ref, wu_ref, wd_ref, gf_ref, o_ref) = refs
    else:
        x_ref, mod_ref, g_ref, wg_ref, wu_ref, wd_ref, gf_ref, o_ref = refs
    expand = _row_expander(x_ref.shape[0], mod_ref.shape[1], rows_per_seq)

    def x_fn():
        x = x_ref[...]
        if merge:
            x = _merge_mixers(x, ol_ref, sa_ref, mb_ref, _mod_row(modm_ref, 2, expand), wuv_ref, wpa_ref, wo_ref)
        return x

    for stage in _ffn_stages(x_fn, mod_ref, g_ref, wg_ref, wu_ref, wd_ref, gf_ref, o_ref, expand, final_norm):
        stage()


def _mod_spec(mod, rows_per_seq):
    if rows_per_seq:
        return pl.BlockSpec(mod.shape, lambda g, i, *_: (0, 0, 0))
    return pl.BlockSpec((None, 3, D_MODEL), lambda g, i, *_: (g, 0, 0))


def _row_spec(tm, width):
    return pl.BlockSpec((None, tm, width), lambda g, i, *_: (g, i, 0))


def _check_tile(x, tm, rows_per_seq):
    G, T, _ = x.shape
    assert T % tm == 0
    if rows_per_seq:
        assert G == 1 and tm == T, "per-sequence modulation is expanded for one tile holding every row"


def _ffn(x, mod, g, wg, wu, wd, g_final, *, rows_per_seq, final_norm, tm, merge_args=None):
    G, T, _ = x.shape
    _check_tile(x, tm, rows_per_seq)
    operands = [x]
    in_specs = [_row_spec(tm, D_MODEL)]
    if merge_args is not None:
        o_lat, sa, mb, mod_mix, w_uv_p, w_pa, w_o = merge_args
        operands += [o_lat, sa, mb, mod_mix, w_uv_p, w_pa, w_o]
        in_specs += [
            pl.BlockSpec((None, N_HEADS, tm, KV_LORA), lambda g, i: (g, 0, i, 0)),
            _row_spec(tm, D_MODEL),
            _row_spec(tm, D_MODEL),
            _mod_spec(mod_mix, rows_per_seq),
            _const_spec(w_uv_p.shape),
            _const_spec(w_pa.shape),
            _const_spec(w_o.shape),
        ]
    operands += [mod, g, wg, wu, wd, g_final]
    in_specs += [
        _mod_spec(mod, rows_per_seq),
        _const_spec((1, D_MODEL)),
        _const_spec(wg.shape),
        _const_spec(wu.shape),
        _const_spec(wd.shape),
        _const_spec((1, D_MODEL)),
    ]
    return pl.pallas_call(
        functools.partial(_ffn_kernel, rows_per_seq=rows_per_seq, final_norm=final_norm,
                          merge=merge_args is not None),
        out_shape=jax.ShapeDtypeStruct(x.shape, F32),
        grid=(G, T // tm),
        in_specs=in_specs,
        out_specs=_row_spec(tm, D_MODEL),
        compiler_params=_params(2),
        name="merge_ffn" if merge_args is not None else "ffn",
    )(*operands)


_MIXIN_SUB_ROWS = 512


def _rope(x, cos2, sin2):
    halves = []
    for k in range(x.shape[1] // LANES):
        xs = x[:, k * LANES:(k + 1) * LANES]
        lane = lax.broadcasted_iota(jnp.int32, xs.shape, 1)
        first = (lane % ROPE_DIM) < (ROPE_DIM // 2)
        swapped = jnp.where(first, pltpu.roll(xs, LANES - ROPE_DIM // 2, 1), pltpu.roll(xs, ROPE_DIM // 2, 1))
        halves.append(swapped)
    swapped = jnp.concatenate(halves, axis=1)
    return x * cos2 + swapped * sin2


def _mixin_kernel(x_ref, mod_ref, g_ref, win_ref, gq_ref, wuq_ref, wuk_ref, gkv_ref, cos_ref, sin_ref,
                  lng_ref, lnb_ref, wmix_ref, bmix_ref, wpb_ref,
                  kv_ref, pe_ref, kcat_ref, kvt_ref, qlat_ref, qpe_ref, sa_ref, mb_ref, gv_ref,
                  *, rows_per_seq, chunk_len):
    tm = x_ref.shape[0]
    row = lax.broadcasted_iota(jnp.int32, (CHUNK, CHUNK), 0)
    col = lax.broadcasted_iota(jnp.int32, (CHUNK, CHUNK), 1)
    keep = (col <= row) & ((row // chunk_len) == (col // chunk_len))
    wm = [jnp.where(keep, wmix_ref[g], 0.0).astype(BF16) for g in range(N_GROUPS)]
    lane = lax.broadcasted_iota(jnp.int32, (CHUNK, LANES), 1)
    low = lane < GROUP_DIM
    bias = bmix_ref[...]

    sub = _MIXIN_SUB_ROWS
    for r in range(tm // sub):
        rows = slice(r * sub, (r + 1) * sub)
        x = x_ref[rows, :]
        expand = _row_expander(sub, mod_ref.shape[1], rows_per_seq, r * sub)
        shift = _mod_row(mod_ref, 0, expand)
        scale = _mod_row(mod_ref, 1, expand)
        nb = _modulated(x, g_ref[...], scale, shift).astype(BF16)
        cos2 = cos_ref[rows, :]
        sin2 = sin_ref[rows, :]

        def proj(off, width, nb=nb):
            return _dot(nb, win_ref[:, off:off + width])

        cqn = _rms(proj(_OFF_CQ, Q_LORA), gq_ref[...]).astype(BF16)
        q = _dot(cqn, wuq_ref[...])
        for h in range(N_HEADS):
            pair = q[:, (h // 2) * LANES:(h // 2 + 1) * LANES].astype(BF16)
            qlat_h = _dot(pair, wuk_ref[h]) * Q_SCALE
            qlat_t = qlat_h.T.astype(BF16)
            for t in range(sub // Q_TILE):
                qlat_ref[r * (sub // Q_TILE) + t, h] = qlat_t[:, t * Q_TILE:(t + 1) * Q_TILE]
        qpe_t = (_rope(q[:, N_HEADS * NOPE_DIM:], cos2, sin2) * Q_SCALE).T.astype(BF16)
        for t in range(sub // Q_TILE):
            qpe_ref[r * (sub // Q_TILE) + t] = qpe_t[:, t * Q_TILE:(t + 1) * Q_TILE]

        kv = _rms(proj(_OFF_CKV, KV_LORA), gkv_ref[...])
        kv_ref[rows, :] = kv
        kpe = _rope(proj(_OFF_KPE, PE_WIDE), cos2, sin2)
        pe_ref[:, rows] = kpe[:, :LANES].T[:ROPE_DIM, :]
        kcat_ref[rows, :] = jnp.concatenate([kv.astype(BF16), kpe.astype(BF16)], axis=1)
        kv_t = kv.T.astype(BF16)
        for c in range(sub // KEY_CHUNK):
            kvt_ref[r * (sub // KEY_CHUNK) + c] = kv_t[:, c * KEY_CHUNK:(c + 1) * KEY_CHUNK]

        u = jax.nn.gelu(proj(_OFF_U, GV))
        v = jax.nn.gelu(proj(_OFF_V, GV))
        mu = jnp.mean(v, axis=-1, keepdims=True)
        vc = v - mu
        var = jnp.mean(vc * vc, axis=-1, keepdims=True)
        v_n = vc * lax.rsqrt(var + EPS) * lng_ref[...] + lnb_ref[...]
        if gv_ref is not None:
            gv_ref[rows, :] = v_n
        vnb = v_n.astype(BF16)
        tiles = []
        for c in range(sub // CHUNK):
            cols = []
            for jp in range(GV // LANES):
                vv = vnb[c * CHUNK:(c + 1) * CHUNK, jp * LANES:(jp + 1) * LANES]
                cols.append(jnp.where(low, _dot(wm[2 * jp], vv), _dot(wm[2 * jp + 1], vv)))
            tiles.append(jnp.concatenate(cols, axis=1) + bias)
        mixed = jnp.concatenate(tiles, axis=0)
        o_b = (u * mixed).astype(BF16)
        mb_ref[rows, :] = (jax.nn.sigmoid(proj(_OFF_GB, D_MODEL)) * _dot(o_b, wpb_ref[...])).astype(BF16)
        sa_ref[rows, :] = jax.nn.sigmoid(proj(_OFF_GA, D_MODEL)).astype(BF16)


def _mixin(x, mod, g_mix, w_in_r, g_q, w_uq_r, w_uk_p, g_kv, cos2, sin2, ln_g, ln_b, w_mix, b_mix, w_pb,
           *, rows_per_seq, chunk_len, want_gv, tm):
    G, T, _ = x.shape
    _check_tile(x, tm, rows_per_seq)
    grid = (G, T // tm)
    out_shape = [
        jax.ShapeDtypeStruct((G, T, KV_LORA), F32),
        jax.ShapeDtypeStruct((G, ROPE_DIM, T), F32),
        jax.ShapeDtypeStruct((G, T, KV_LORA + PE_WIDE), BF16),
        jax.ShapeDtypeStruct((G, T // KEY_CHUNK, KV_LORA, KEY_CHUNK), BF16),
        jax.ShapeDtypeStruct((G, T // Q_TILE, N_HEADS, KV_LORA, Q_TILE), BF16),
        jax.ShapeDtypeStruct((G, T // Q_TILE, PE_WIDE, Q_TILE), BF16),
        jax.ShapeDtypeStruct((G, T, D_MODEL), BF16),
        jax.ShapeDtypeStruct((G, T, D_MODEL), BF16),
    ]
    out_specs = [
        _row_spec(tm, KV_LORA),
        pl.BlockSpec((None, ROPE_DIM, tm), lambda g, i: (g, 0, i)),
        _row_spec(tm, KV_LORA + PE_WIDE),
        pl.BlockSpec((None, tm // KEY_CHUNK, KV_LORA, KEY_CHUNK), lambda g, i: (g, i, 0, 0)),
        pl.BlockSpec((None, tm // Q_TILE, N_HEADS, KV_LORA, Q_TILE), lambda g, i: (g, i, 0, 0, 0)),
        pl.BlockSpec((None, tm // Q_TILE, PE_WIDE, Q_TILE), lambda g, i: (g, i, 0, 0)),
        _row_spec(tm, D_MODEL),
        _row_spec(tm, D_MODEL),
    ]
    if want_gv:
        out_shape.append(jax.ShapeDtypeStruct((G, T, GV), F32))
        out_specs.append(_row_spec(tm, GV))
        body = functools.partial(_mixin_kernel, rows_per_seq=rows_per_seq, chunk_len=chunk_len)
    else:
        def body(*refs):
            _mixin_kernel(*refs, None, rows_per_seq=rows_per_seq, chunk_len=chunk_len)
    return pl.pallas_call(
        body,
        out_shape=out_shape,
        grid=grid,
        in_specs=[
            _row_spec(tm, D_MODEL),
            _mod_spec(mod, rows_per_seq),
            _const_spec((1, D_MODEL)),
            _const_spec(w_in_r.shape),
            _const_spec((1, Q_LORA)),
            _const_spec(w_uq_r.shape),
            _const_spec(w_uk_p.shape),
            _const_spec((1, KV_LORA)),
            pl.BlockSpec((tm, PE_WIDE), lambda g, i: (i, 0)),
            pl.BlockSpec((tm, PE_WIDE), lambda g, i: (i, 0)),
            _const_spec((1, GV)),
            _const_spec((1, GV)),
            _const_spec(w_mix.shape),
            _const_spec(b_mix.shape),
            _const_spec(w_pb.shape),
        ],
        out_specs=out_specs,
        compiler_params=_params(2),
        name="mixin",
    )(x, mod, g_mix, w_in_r, g_q, w_uq_r, w_uk_p, g_kv, cos2, sin2, ln_g, ln_b, w_mix, b_mix, w_pb)


_TQ = Q_TILE
_COL_BLOCK = 256


def _pattn_kernel(qlat_ref, qpe_ref, kcat_ref, kvt_ref, o_ref, qt_sc, s0_sc, s1_sc, m_sc, l_sc, acc_sc):
    i = pl.program_id(1)
    tk = KEY_CHUNK
    head_of_row = lax.broadcasted_iota(jnp.int32, (PE_WIDE, _TQ), 0) // ROPE_DIM
    qp = qpe_ref[...]
    for h in range(N_HEADS):
        qt_sc[0:KV_LORA, h * _TQ:(h + 1) * _TQ] = qlat_ref[h]
        qt_sc[KV_LORA:, h * _TQ:(h + 1) * _TQ] = jnp.where(head_of_row == h, qp, jnp.zeros_like(qp))
    m_sc[...] = jnp.full(m_sc.shape, -1e30, F32)
    l_sc[...] = jnp.zeros(l_sc.shape, F32)
    acc_sc[...] = jnp.zeros(acc_sc.shape, F32)

    def scores(j, s_ref):
        k = kcat_ref[pl.ds(pl.multiple_of(j * tk, tk), tk), :]
        s_ref[...] = _dot(k, qt_sc[...])

    def accumulate(j, s_ref, masked):
        v_t = kvt_ref[j]
        for cb in range(N_HEADS * _TQ // _COL_BLOCK):
            cols = slice(cb * _COL_BLOCK, (cb + 1) * _COL_BLOCK)
            s = s_ref[:, cols]
            if masked:
                col = cb * _COL_BLOCK + lax.broadcasted_iota(jnp.int32, s.shape, 1)
                q_pos = i * _TQ + col % _TQ
                k_pos = j * tk + lax.broadcasted_iota(jnp.int32, s.shape, 0)
                s = jnp.where(k_pos <= q_pos, s, -jnp.inf)
            m_prev = m_sc[:, cols]
            m_new = jnp.maximum(m_prev, jnp.max(s, axis=0, keepdims=True))
            alpha = jnp.exp2(m_prev - m_new)
            p = jnp.exp2(s - m_new)
            l_sc[:, cols] = alpha * l_sc[:, cols] + jnp.sum(p, axis=0, keepdims=True)
            acc_sc[:, cols] = alpha * acc_sc[:, cols] + _dot(v_t, p.astype(BF16))
            m_sc[:, cols] = m_new

    n_full = (i * _TQ) // tk
    scores(0, s0_sc)

    def pair(p, carry):
        j = 2 * p
        scores(j + 1, s1_sc)
        accumulate(j, s0_sc, False)
        scores(j + 2, s0_sc)
        accumulate(j + 1, s1_sc, False)
        return carry

    lax.fori_loop(0, n_full // 2, pair, 0)

    assert _TQ <= tk, "only chunk n_full may hold masked keys"

    @pl.when(n_full % 2 == 0)
    def _():
        accumulate(n_full, s0_sc, True)

    @pl.when(n_full % 2 == 1)
    def _():
        scores(n_full, s1_sc)
        accumulate(n_full - 1, s0_sc, False)
        accumulate(n_full, s1_sc, True)

    o_t = acc_sc[...] / l_sc[...]
    for h in range(N_HEADS):
        o_ref[h] = o_t[:, h * _TQ:(h + 1) * _TQ].T.astype(BF16)


def _pattn(q_lat_t, q_pe_t, kcat, kv_t):
    B, n_tiles, H, C, _ = q_lat_t.shape
    S = n_tiles * _TQ
    cols = H * _TQ
    return pl.pallas_call(
        _pattn_kernel,
        out_shape=jax.ShapeDtypeStruct((B, H, S, C), BF16),
        grid=(B, S // _TQ),
        in_specs=[
            pl.BlockSpec((None, None, H, C, _TQ), lambda b, i: (b, i, 0, 0, 0)),
            pl.BlockSpec((None, None, PE_WIDE, _TQ), lambda b, i: (b, i, 0, 0)),
            pl.BlockSpec((None, S, C + PE_WIDE), lambda b, i: (b, 0, 0)),
            pl.BlockSpec((None, S // KEY_CHUNK, C, KEY_CHUNK), lambda b, i: (b, 0, 0, 0)),
        ],
        out_specs=pl.BlockSpec((None, H, _TQ, C), lambda b, i: (b, 0, i, 0)),
        scratch_shapes=[
            pltpu.VMEM((C + PE_WIDE, cols), BF16),
            pltpu.VMEM((KEY_CHUNK, cols), F32),
            pltpu.VMEM((KEY_CHUNK, cols), F32),
            pltpu.VMEM((1, cols), F32),
            pltpu.VMEM((1, cols), F32),
            pltpu.VMEM((C, cols), F32),
        ],
        compiler_params=_params(2),
        name="pattn",
    )(q_lat_t, q_pe_t, kcat, kv_t)


_PAGES_PER_CHUNK = 64


_DMA_SLOTS = 3


def _sattn_kernel(pt_ref, ql_ref, qp_ref, kvn_ref, pen_ref, ckv_hbm, cpet_hbm, o_ref,
                  kvbuf, pebuf, kvb16_0, kvb16_1, s0_sc, s1_sc, kt_sc, m_sc, l_sc, acc_sc, sem, *, n_pages, t_new):
    pc = _PAGES_PER_CHUNK
    n_chunks = n_pages // pc
    n_seq, rows, _ = ql_ref.shape
    total = n_seq * n_chunks
    assert n_chunks % 2 == 0 and total > _DMA_SLOTS
    work = ((kvb16_0, s0_sc), (kvb16_1, s1_sc))

    def copies(g, slot):
        b = g // n_chunks
        c = g % n_chunks
        out = []
        for p in range(pc):
            page = pt_ref[b, c * pc + p]
            dst = pl.ds(p * PAGE_SIZE, PAGE_SIZE)
            out.append(pltpu.make_async_copy(ckv_hbm.at[page], kvbuf.at[slot, dst], sem.at[0, slot]))
            out.append(pltpu.make_async_copy(cpet_hbm.at[page], pebuf.at[slot, :, dst], sem.at[1, slot]))
        return out

    def start(g):
        for cp in copies(g, g % _DMA_SLOTS):
            cp.start()

    def wait(g):
        for cp in copies(g, g % _DMA_SLOTS):
            cp.wait()

    def scores(g, w):
        kvb16, s_sc = work[w]
        b = g // n_chunks
        slot = g % _DMA_SLOTS
        wait(g)
        kvb = kvbuf[slot].astype(BF16)
        kvb16[...] = kvb
        peb = pebuf[slot].astype(BF16)
        kt_sc[...] = kvb.T
        s_sc[...] = _dot(ql_ref[b], kt_sc[...]) + _dot(qp_ref[b], peb)

    def reset_state():
        m_sc[...] = jnp.full(m_sc.shape, -1e30, F32)
        l_sc[...] = jnp.zeros(l_sc.shape, F32)
        acc_sc[...] = jnp.zeros(acc_sc.shape, F32)

    def accumulate(w):
        kvb16, s_sc = work[w]
        s = s_sc[...]
        m_prev = m_sc[...]
        m_new = jnp.maximum(m_prev, jnp.max(s, axis=-1, keepdims=True))
        alpha = jnp.exp2(m_prev - m_new)
        p = jnp.exp2(s - m_new)
        l_sc[...] = alpha * l_sc[...] + jnp.sum(p, axis=-1, keepdims=True)
        acc_sc[...] = alpha * acc_sc[...] + _dot(p.astype(BF16), kvb16[...])
        m_sc[...] = m_new

    def finish(b):
        qlf = ql_ref[b].astype(F32)
        qpf = qp_ref[b].astype(F32)
        kvn = kvn_ref[b]
        pen = pen_ref[b]
        t_of_row = lax.broadcasted_iota(jnp.int32, (rows, 1), 0) % t_new
        s_new = []
        for t in range(t_new):
            st = (jnp.sum(qlf * kvn[t:t + 1, :], axis=-1, keepdims=True)
                  + jnp.sum(qpf * pen[t:t + 1, :], axis=-1, keepdims=True))
            s_new.append(jnp.where(t <= t_of_row, st, -jnp.inf))
        m_prev = m_sc[...]
        m_new = m_prev
        for st in s_new:
            m_new = jnp.maximum(m_new, st)
        alpha = jnp.exp2(m_prev - m_new)
        l_new = alpha * l_sc[...]
        acc = alpha * acc_sc[...]
        for t, st in enumerate(s_new):
            pt = jnp.exp2(st - m_new)
            l_new = l_new + pt
            acc = acc + pt * kvn[t:t + 1, :]
        o_ref[b] = (acc / l_new).astype(o_ref.dtype)
        reset_state()

    reset_state()
    for g0 in range(_DMA_SLOTS):
        start(g0)
    scores(0, 0)

    def body(k, carry):
        g = 2 * k + 1
        start(g + _DMA_SLOTS - 1)
        scores(g, 1)
        accumulate(0)
        start(g + _DMA_SLOTS)
        scores(g + 1, 0)
        accumulate(1)

        @pl.when((g + 1) % n_chunks == 0)
        def _():
            finish((g + 1) // n_chunks - 1)

        return carry

    lax.fori_loop(0, total // 2 - 1 + pl.program_id(0), body, 0)
    scores(total - 1, 1)
    accumulate(0)
    accumulate(1)
    finish(n_seq - 1)
    wait(total)


def _sattn(page_table, q_lat, q_pe, kv_new, pe_new, cache_kv, cache_pe_t):
    DB, rows, C = q_lat.shape
    t_new = kv_new.shape[1]
    n_pages = page_table.shape[1]
    keys = _PAGES_PER_CHUNK * PAGE_SIZE
    page_table = jnp.concatenate([page_table, page_table[:1]], axis=0)
    grid_spec = pltpu.PrefetchScalarGridSpec(
        num_scalar_prefetch=1,
        grid=(1,),
        in_specs=[
            _const_spec(q_lat.shape),
            _const_spec(q_pe.shape),
            _const_spec(kv_new.shape),
            _const_spec(pe_new.shape),
            pl.BlockSpec(memory_space=pl.ANY),
            pl.BlockSpec(memory_space=pl.ANY),
        ],
        out_specs=pl.BlockSpec((DB, rows, C), lambda i, pt: (0, 0, 0)),
        scratch_shapes=[
            pltpu.VMEM((_DMA_SLOTS, keys, C), F32),
            pltpu.VMEM((_DMA_SLOTS, ROPE_DIM, keys), F32),
            pltpu.VMEM((keys, C), BF16),
            pltpu.VMEM((keys, C), BF16),
            pltpu.VMEM((rows, keys), F32),
            pltpu.VMEM((rows, keys), F32),
            pltpu.VMEM((C, keys), BF16),
            pltpu.VMEM((rows, 1), F32),
            pltpu.VMEM((rows, 1), F32),
            pltpu.VMEM((rows, C), F32),
            pltpu.SemaphoreType.DMA((2, _DMA_SLOTS)),
        ],
    )
    return pl.pallas_call(
        functools.partial(_sattn_kernel, n_pages=n_pages, t_new=t_new),
        out_shape=jax.ShapeDtypeStruct((DB, rows, C), BF16),
        grid_spec=grid_spec,
        compiler_params=_params(1),
        name="sattn",
    )(page_table, q_lat, q_pe, kv_new, pe_new, cache_kv, cache_pe_t)


def _rope_tables(pos):
    half = ROPE_DIM // 2
    freqs = ROPE_THETA ** (-2.0 * jnp.arange(half, dtype=F32) / ROPE_DIM)
    ang = pos.astype(F32)[:, None] * freqs[None, :]
    cos = jnp.cos(ang)
    sin = jnp.sin(ang)
    cos2 = jnp.tile(jnp.concatenate([cos, cos], axis=1), (1, N_HEADS))
    sin2 = jnp.tile(jnp.concatenate([-sin, sin], axis=1), (1, N_HEADS))
    return cos2, sin2


def _pair_pad(w, axis):
    z = jnp.zeros_like(w)
    even = jnp.concatenate([w, z], axis=axis)
    odd = jnp.concatenate([z, w], axis=axis)
    sel = (jnp.arange(w.shape[0]) % 2 == 0).reshape(-1, 1, 1)
    return jnp.where(sel, even, odd)


def kernel(x_prompt, x_sample, cache_kv, cache_pe, page_table, c_prompt, c_sample, w_ada, b_ada, g_ffn1, w1_gate, w1_up, w1_down, g_mix, w_in, g_q, w_uq, g_kv, w_uk, w_uv, ln_v_g, ln_v_b, w_s, b_s, w_pa, w_pb, w_o, g_ffn2, w2_gate, w2_up, w2_down, g_final):
    B, S, D = x_prompt.shape
    DB, T, _ = x_sample.shape
    depth = w_ada.shape[0]
    assert depth == 1
    l = 0
    R = DB * T

    row = lambda v: v.reshape(1, -1)
    wi = w_in[l]
    seg = lambda a, n: wi[:, a:a + n]
    o_ckv = Q_LORA
    o_kpe = o_ckv + KV_LORA
    o_u = o_kpe + ROPE_DIM
    o_v = o_u + GV
    o_ga = o_v + GV
    o_gb = o_ga + D_MODEL
    w_in_r = jnp.concatenate([
        seg(0, Q_LORA), seg(o_ckv, KV_LORA), jnp.tile(seg(o_kpe, ROPE_DIM), (1, N_HEADS)),
        seg(o_u, GV), seg(o_v, GV), seg(o_ga, D_MODEL), seg(o_gb, D_MODEL)], axis=1).astype(BF16)
    wq = w_uq[l].reshape(Q_LORA, N_HEADS, NOPE_DIM + ROPE_DIM)
    w_uq_r = jnp.concatenate([wq[:, :, :NOPE_DIM].reshape(Q_LORA, -1),
                              wq[:, :, NOPE_DIM:].reshape(Q_LORA, -1)], axis=1).astype(BF16)
    w_uk_p = _pair_pad(jnp.transpose(w_uk[l], (1, 2, 0)), 1).astype(BF16)
    w_uv_p = _pair_pad(jnp.transpose(w_uv[l], (1, 0, 2)), 2).astype(BF16)
    bf = lambda w: w[l].astype(BF16)
    w1g, w1u, w1d = bf(w1_gate), bf(w1_up), bf(w1_down)
    w2g, w2u, w2d = bf(w2_gate), bf(w2_up), bf(w2_down)
    wpa, wpb, wo = bf(w_pa), bf(w_pb), bf(w_o)
    gf = row(g_final)

    w_mix_p = w_s[l]
    b_mix_p = jnp.repeat(b_s[l].T, GROUP_DIM, axis=1)
    reps = CHUNK // T
    w_mix_s = jnp.tile(w_s[l][:, :T, :T], (1, reps, reps))
    b_mix_s = jnp.repeat(jnp.tile(b_s[l][:, :T].T, (reps, 1)), GROUP_DIM, axis=1)

    mod = _ada(jnp.concatenate([c_prompt, c_sample], axis=0), w_ada[l], b_ada[l])
    mod_p = mod[:B].reshape(B, N_SUB, 3, D)
    mod_s = jnp.transpose(mod[B:].reshape(DB, N_SUB, 3, D), (1, 2, 0, 3))

    past_len = page_table.shape[1] * PAGE_SIZE
    cos_p, sin_p = _rope_tables(jnp.arange(S))
    cos_s, sin_s = _rope_tables(past_len + jnp.arange(R) % T)

    def mixin(x1, mod, rows_per_seq, tm, chunk_len, cos2, sin2, w_mix, b_mix, want_gv):
        return _mixin(x1, mod, row(g_mix[l]), w_in_r, row(g_q[l]), w_uq_r, w_uk_p, row(g_kv[l]),
                      cos2, sin2, row(ln_v_g[l]), row(ln_v_b[l]), w_mix, b_mix, wpb,
                      rows_per_seq=rows_per_seq, chunk_len=chunk_len, want_gv=want_gv, tm=tm)

    def merge_ffn2(x1, mod_mix, mod_ffn, o_lat, sa, mb, rows_per_seq, tm):
        return _ffn(x1, mod_ffn, row(g_ffn2[l]), w2g, w2u, w2d, gf,
                    rows_per_seq=rows_per_seq, final_norm=True, tm=tm,
                    merge_args=(o_lat, sa, mb, mod_mix, w_uv_p, wpa, wo))

    tm_p = 512
    x1_s = _ffn(x_sample.reshape(1, R, D), mod_s[0], row(g_ffn1[l]), w1g, w1u, w1d, gf,
                rows_per_seq=T, final_norm=False, tm=R)
    kv_s, pe_t_s, _, _, q_lat_t, q_pe_t, sa_s, mb_s, gv_s = mixin(x1_s, mod_s[1], T, R, T, cos_s, sin_s,
                                                                  w_mix_s, b_mix_s, True)
    pe_s = jnp.swapaxes(pe_t_s, 1, 2)
    ql = jnp.transpose(q_lat_t[0], (0, 3, 1, 2)).reshape(DB, T, N_HEADS, KV_LORA)
    ql = jnp.transpose(ql, (0, 2, 1, 3)).reshape(DB, N_HEADS * T, KV_LORA)
    qp = jnp.transpose(q_pe_t[0], (0, 2, 1)).reshape(DB, T, N_HEADS, ROPE_DIM)
    qp = jnp.transpose(qp, (0, 2, 1, 3)).reshape(DB, N_HEADS * T, ROPE_DIM)

    cache_pe_t = jnp.swapaxes(cache_pe[l], 1, 2)
    o_s = _sattn(page_table, ql, qp, kv_s.reshape(DB, T, KV_LORA), pe_s.reshape(DB, T, ROPE_DIM),
                 cache_kv[l], cache_pe_t)

    x1_p = _ffn(x_prompt, mod_p[:, 0], row(g_ffn1[l]), w1g, w1u, w1d, gf,
                rows_per_seq=0, final_norm=False, tm=tm_p)
    kv_p, pe_t_p, kcat, kv_t, q_lat_t, q_pe_t, sa_p, mb_p = mixin(x1_p, mod_p[:, 1], 0, tm_p, CHUNK, cos_p, sin_p,
                                                                  w_mix_p, b_mix_p, False)
    pe_p = jnp.swapaxes(pe_t_p, 1, 2)
    o_lat_p = _pattn(q_lat_t, q_pe_t, kcat, kv_t)
    y_p = merge_ffn2(x1_p, mod_p[:, 1], mod_p[:, 2], o_lat_p, sa_p, mb_p, 0, tm_p)

    o_lat_s = jnp.transpose(o_s.reshape(DB, N_HEADS, T, KV_LORA), (1, 0, 2, 3)).reshape(1, N_HEADS, R, KV_LORA)
    y_s = merge_ffn2(x1_s, mod_s[1], mod_s[2], o_lat_s, sa_s, mb_s, T, R)

    return (y_p, y_s.reshape(DB, T, D),
            kv_p.reshape(1, B, S, KV_LORA), pe_p.reshape(1, B, S, ROPE_DIM),
            kv_s.reshape(1, DB, T, KV_LORA), pe_s.reshape(1, DB, T, ROPE_DIM),
            gv_s.reshape(1, DB, T, GV))
```
